```python
import jax, jax.numpy as jnp
from jax import lax
import numpy as np

D_MODEL = 1024
BATCH = 2
SEQ = 16384
DEPTH = 4

N_MEM = 256
XA_HEADS = 4
XA_HEAD_DIM = D_MODEL // XA_HEADS
GM_CHUNK = 128
GM_INNER = D_MODEL
GM_GROUPS = 8
GM_GROUP_DIM = GM_INNER // GM_GROUPS
RW_HEAD_DIM = 64
RW_HEADS = D_MODEL // RW_HEAD_DIM
RW_DECAY_LORA = max(32, int(round(1.8 * D_MODEL ** 0.5 / 32)) * 32)
RW_AAA_LORA = max(32, int(round(1.8 * D_MODEL ** 0.5 / 32)) * 32)
RW_MV_LORA = max(32, int(round(1.3 * D_MODEL ** 0.5 / 32)) * 32)
RW_GATE_LORA = max(32, int(round(0.6 * D_MODEL ** 0.8 / 32)) * 32)
RW_GN_EPS = 64e-5
NORM_EPS = 1e-12
D_FF = -(-8 * D_MODEL // (3 * 256)) * 256
RMS_EPS = 1e-6
LN_EPS = 1e-5

kernel_name = "gmlp_rwkv7_interleaved_memxattn_sandwich"


def rms_norm(x, g):
    xf = x.astype(jnp.float32)
    y = xf * lax.rsqrt(jnp.mean(xf * xf, axis=-1, keepdims=True) + RMS_EPS)
    return (y * g.astype(jnp.float32)).astype(x.dtype)


def layer_norm(x, g, b):
    xf = x.astype(jnp.float32)
    mu = jnp.mean(xf, axis=-1, keepdims=True)
    var = jnp.mean(jnp.square(xf - mu), axis=-1, keepdims=True)
    y = (xf - mu) * lax.rsqrt(var + LN_EPS) * g.astype(jnp.float32) + b.astype(jnp.float32)
    return y.astype(x.dtype)


def gmlp_spatial_gating(x, w_in, b_in, ln_g, ln_b, w_s, b_s, w_out):
    B, S, _ = x.shape
    h = jax.nn.gelu(x @ w_in + b_in, approximate=False)
    u, v = jnp.split(h, 2, axis=-1)
    v = layer_norm(v, ln_g, ln_b)
    v = v.reshape(B, S // GM_CHUNK, GM_CHUNK, GM_GROUPS, GM_GROUP_DIM)
    w_causal = jnp.tril(w_s)
    mixed = jnp.einsum('gts,bnsgc->bntgc', w_causal, v) + b_s.T[:, :, None]
    return (u * mixed.reshape(B, S, GM_INNER)) @ w_out


def token_shift(x):
    return jnp.pad(x[:, :-1], ((0, 0), (1, 0), (0, 0)))


def rwkv7_recurrence(r, w, k, v, a, b):
    B, T, H, N = r.shape

    def step(S, inp):
        r_t, w_t, k_t, v_t, a_t, b_t = inp
        sa = jnp.einsum('bhij,bhj->bhi', S, a_t)
        S = S * w_t[:, :, None, :] + sa[..., None] * b_t[:, :, None, :] + v_t[..., None] * k_t[:, :, None, :]
        return S, jnp.einsum('bhij,bhj->bhi', S, r_t)

    xs = tuple(jnp.moveaxis(z, 1, 0) for z in (r, w, k, v, a, b))
    S0 = jnp.zeros((B, H, N, N), jnp.float32)
    _, y = lax.scan(step, S0, xs)
    return jnp.moveaxis(y, 0, 1)


def rwkv7_time_mix(x, v_first, mix, w_rkv, w0, w1, w2, a0, a1, a2, g1, g2,
                   k_k, k_a, r_k, ln_g, ln_b, w_o, v_res):
    B, T, C = x.shape
    xx = token_shift(x) - x
    xm = x[None] + xx[None] * mix[:, None, None, :]
    r, k, v = jnp.einsum('pbtc,pcd->pbtd', xm[:3], w_rkv)
    xv, xw, xa, xg = xm[2], xm[3], xm[4], xm[5]
    w = -jax.nn.softplus(-(w0 + jnp.tanh(xw @ w1) @ w2)) - 0.5
    a = jax.nn.sigmoid(a0 + (xa @ a1) @ a2)
    g = jax.nn.sigmoid(xg @ g1) @ g2
    if v_res is None:
        v_first = v
    else:
        v0, v1, v2 = v_res
        v = v + (v_first - v) * jax.nn.sigmoid(v0 + (xv @ v1) @ v2)

    def heads(z):
        return z.astype(jnp.float32).reshape(B, T, RW_HEADS, RW_HEAD_DIM)

    kk = heads(k * k_k)
    kk = kk / jnp.maximum(jnp.linalg.norm(kk, axis=-1, keepdims=True), NORM_EPS)
    k = k * (1.0 + (a - 1.0) * k_a)
    rh, kh, vh, ah = heads(r), heads(k), heads(v), heads(a)
    decay = jnp.exp(-jnp.exp(heads(w)))
    y = rwkv7_recurrence(rh, decay, kh, vh, -kk, kk * ah)
    mu = jnp.mean(y, axis=-1, keepdims=True)
    var = jnp.mean(jnp.square(y - mu), axis=-1, keepdims=True)
    y = (y - mu) * lax.rsqrt(var + RW_GN_EPS)
    y = y * ln_g.astype(jnp.float32).reshape(RW_HEADS, RW_HEAD_DIM) + ln_b.astype(jnp.float32).reshape(RW_HEADS, RW_HEAD_DIM)
    y = y + jnp.sum(rh * kh * r_k.astype(jnp.float32), axis=-1, keepdims=True) * vh
    y = y.reshape(B, T, C).astype(x.dtype)
    return (y * g) @ w_o, v_first


def memory_cross_attention(x, m, wq, wkv, wo):
    B, S, _ = x.shape
    M = m.shape[1]
    q = (x @ wq).reshape(B, S, XA_HEADS, XA_HEAD_DIM)
    k, v = jnp.split(m @ wkv, 2, axis=-1)
    k = k.reshape(B, M, XA_HEADS, XA_HEAD_DIM)
    v = v.reshape(B, M, XA_HEADS, XA_HEAD_DIM)
    scores = jnp.einsum('bshd,bmhd->bhsm', q, k).astype(jnp.float32) * (XA_HEAD_DIM ** -0.5)
    p = jax.nn.softmax(scores, axis=-1).astype(x.dtype)
    o = jnp.einsum('bhsm,bmhd->bshd', p, v).reshape(B, S, D_MODEL)
    return o @ wo


def swiglu_ffn(x, w_in, w_out):
    gate, up = jnp.split(x @ w_in, 2, axis=-1)
    return (jax.nn.silu(gate) * up) @ w_out


def setup_inputs(seed: int = 0) -> dict:
    key = jax.random.key(seed)
    ks = jax.random.split(key, 48)
    ctr = [0]

    def nxt():
        ctr[0] += 1
        return ks[ctr[0] - 1]

    def nrm(shape, scale):
        return scale * jax.random.normal(nxt(), shape, jnp.float32)

    def unif(shape, lo, hi):
        return jax.random.uniform(nxt(), shape, jnp.float32, lo, hi)

    D = D_MODEL
    n_a = (DEPTH + 1) // 2
    n_b = DEPTH // 2
    n_vr = max(n_b - 1, 0)
    return {
        "x": nrm((BATCH, SEQ, D), 1.0),
        "mem": nrm((BATCH, N_MEM, D), 1.0),
        "norm_gains": 1.0 + nrm((DEPTH, 6, D), 0.02),
        "mem_norm_gains": 1.0 + nrm((DEPTH, D), 0.02),
        "xa_wq": nrm((DEPTH, D, D), D ** -0.5),
        "xa_wkv": nrm((DEPTH, D, 2 * D), D ** -0.5),
        "xa_wo": nrm((DEPTH, D, D), D ** -0.5),
        "ffn_w_in": nrm((DEPTH, D, 2 * D_FF), D ** -0.5),
        "ffn_w_out": nrm((DEPTH, D_FF, D), D_FF ** -0.5),
        "gm_w_in": nrm((n_a, D, 2 * GM_INNER), D ** -0.5),
        "gm_b_in": nrm((n_a, 2 * GM_INNER), 0.02),
        "gm_ln_g": 1.0 + nrm((n_a, GM_INNER), 0.02),
        "gm_ln_b": nrm((n_a, GM_INNER), 0.02),
        "gm_w_s": nrm((n_a, GM_GROUPS, GM_CHUNK, GM_CHUNK), GM_CHUNK ** -0.5),
        "gm_b_s": 1.0 + nrm((n_a, GM_GROUPS, GM_CHUNK), 0.02),
        "gm_w_out": nrm((n_a, GM_INNER, D), GM_INNER ** -0.5),
        "rw_mix": unif((n_b, 6, D), 0.0, 1.0),
        "rw_w_rkv": nrm((n_b, 3, D, D), D ** -0.5),
        "rw_w0": unif((n_b, D), -6.0, -1.0),
        "rw_w1": nrm((n_b, D, RW_DECAY_LORA), D ** -0.5),
        "rw_w2": nrm((n_b, RW_DECAY_LORA, D), 0.1 * RW_DECAY_LORA ** -0.5),
        "rw_a0": nrm((n_b, D), 0.1),
        "rw_a1": nrm((n_b, D, RW_AAA_LORA), D ** -0.5),
        "rw_a2": nrm((n_b, RW_AAA_LORA, D), RW_AAA_LORA ** -0.5),
        "rw_g1": nrm((n_b, D, RW_GATE_LORA), D ** -0.5),
        "rw_g2": nrm((n_b, RW_GATE_LORA, D), RW_GATE_LORA ** -0.5),
        "rw_k_k": 0.85 + nrm((n_b, D), 0.02),
        "rw_k_a": 1.0 + nrm((n_b, D), 0.02),
        "rw_r_k": nrm((n_b, RW_HEADS, RW_HEAD_DIM), 0.1),
        "rw_ln_g": 1.0 + nrm((n_b, D), 0.02),
        "rw_ln_b": nrm((n_b, D), 0.02),
        "rw_w_o": nrm((n_b, D, D), D ** -0.5),
        "rw_v0": 1.0 + nrm((n_vr, D), 0.1),
        "rw_v1": nrm((n_vr, D, RW_MV_LORA), D ** -0.5),
        "rw_v2": nrm((n_vr, RW_MV_LORA, D), RW_MV_LORA ** -0.5),
    }


def reference(x, mem, norm_gains, mem_norm_gains, xa_wq, xa_wkv, xa_wo, ffn_w_in, ffn_w_out,
              gm_w_in, gm_b_in, gm_ln_g, gm_ln_b, gm_w_s, gm_b_s, gm_w_out,
              rw_mix, rw_w_rkv, rw_w0, rw_w1, rw_w2, rw_a0, rw_a1, rw_a2, rw_g1, rw_g2,
              rw_k_k, rw_k_a, rw_r_k, rw_ln_g, rw_ln_b, rw_w_o, rw_v0, rw_v1, rw_v2):
    v_first = None
    for i in range(DEPTH):
        g = norm_gains[i]
        j = i // 2
        h = rms_norm(x, g[0])
        if i % 2 == 0:
            h = gmlp_spatial_gating(h, gm_w_in[j], gm_b_in[j], gm_ln_g[j], gm_ln_b[j],
                                    gm_w_s[j], gm_b_s[j], gm_w_out[j])
        else:
            v_res = (rw_v0[j - 1], rw_v1[j - 1], rw_v2[j - 1]) if j > 0 else None
            h, v_first = rwkv7_time_mix(h, v_first, rw_mix[j], rw_w_rkv[j], rw_w0[j], rw_w1[j], rw_w2[j],
                                        rw_a0[j], rw_a1[j], rw_a2[j], rw_g1[j], rw_g2[j],
                                        rw_k_k[j], rw_k_a[j], rw_r_k[j], rw_ln_g[j], rw_ln_b[j],
                                        rw_w_o[j], v_res)
        x = x + rms_norm(h, g[1])
        m = rms_norm(mem, mem_norm_gains[i])
        h = memory_cross_attention(rms_norm(x, g[2]), m, xa_wq[i], xa_wkv[i], xa_wo[i])
        x = x + rms_norm(h, g[3])
        h = swiglu_ffn(rms_norm(x, g[4]), ffn_w_in[i], ffn_w_out[i])
        x = x + rms_norm(h, g[5])
    return x
```

```python
import functools

import jax
import jax.numpy as jnp
from jax import lax
from jax.experimental import pallas as pl
from jax.experimental.pallas import tpu as pltpu

F32 = jnp.float32
BF16 = jnp.bfloat16

D_MODEL = 1024
XA_HEADS = 4
XA_HEAD_DIM = D_MODEL // XA_HEADS
GM_CHUNK = 128
GM_GROUPS = 8
GM_GROUP_DIM = D_MODEL // GM_GROUPS
RW_HEAD_DIM = 64
RW_HEADS = D_MODEL // RW_HEAD_DIM
RW_GN_EPS = 64e-5
NORM_EPS = 1e-12
RMS_EPS = 1e-6
LN_EPS = 1e-5

LANES = 128
SUBLANES = 8
RW_CHUNK = 64
RW_PAIR = LANES // RW_HEAD_DIM
RW_GROUPS = D_MODEL // LANES
RW_STACK = RW_PAIR * RW_CHUNK
VMEM_LIMIT = 56 * 1024 * 1024

HI = lax.Precision.HIGHEST


def _dot(a, b):
    return jnp.dot(a, b, preferred_element_type=F32)


def _dot_nt(a, b):
    return lax.dot_general(a, b, (((1,), (1,)), ((), ())), preferred_element_type=F32)


def _dot_tn(a, b):
    return lax.dot_general(a, b, (((0,), (0,)), ((), ())), preferred_element_type=F32)


def _rms(x, g):
    return x * lax.rsqrt(jnp.mean(x * x, axis=-1, keepdims=True) + RMS_EPS) * g


def _sigmoid(x):
    return 1.0 / (1.0 + jnp.exp(-x))


def _const_spec(shape):
    nd = len(shape)
    return pl.BlockSpec(shape, lambda *_: (0,) * nd, pipeline_mode=pl.Buffered(1))


def _params(n_axes=1):
    return pltpu.CompilerParams(dimension_semantics=("arbitrary",) * n_axes,
                                vmem_limit_bytes=VMEM_LIMIT)


def _memkv_kernel(mem_ref, g_ref, w_ref, o_ref):
    m = _rms(mem_ref[0], g_ref[0]).astype(BF16)
    o_ref[0, 0] = _dot(m, w_ref[0]).astype(BF16)


def _memkv(mem, gains, wkv):
    depth = wkv.shape[0]
    b, m, d = mem.shape
    return pl.pallas_call(
        _memkv_kernel,
        grid=(depth, b),
        in_specs=[
            pl.BlockSpec((1, m, d), lambda i, j: (j, 0, 0)),
            pl.BlockSpec((1, 1, d), lambda i, j: (i, 0, 0)),
            pl.BlockSpec((1, d, 2 * d), lambda i, j: (i, 0, 0)),
        ],
        out_specs=pl.BlockSpec((1, 1, m, 2 * d), lambda i, j: (i, j, 0, 0)),
        out_shape=jax.ShapeDtypeStruct((depth, b, m, 2 * d), BF16),
        compiler_params=_params(2),
        name="memkv",
    )(mem, gains.reshape(depth, 1, d), wkv)


def _xattn_kernel(x_ref, g_ref, wq_ref, kv_ref, wo_ref, o_ref):
    x = x_ref[...]
    h = _rms(x, g_ref[0:1]).astype(BF16)
    q = _dot(h, wq_ref[...]) * (XA_HEAD_DIM ** -0.5)
    heads = []
    for hd in range(XA_HEADS):
        lo = hd * XA_HEAD_DIM
        qh = q[:, lo:lo + XA_HEAD_DIM].astype(BF16)
        kh = kv_ref[0, :, lo:lo + XA_HEAD_DIM]
        vh = kv_ref[0, :, D_MODEL + lo:D_MODEL + lo + XA_HEAD_DIM]
        s = _dot_nt(qh, kh)
        p = jnp.exp(s - jnp.max(s, axis=-1, keepdims=True))
        p = p / jnp.sum(p, axis=-1, keepdims=True)
        heads.append(_dot(p.astype(BF16), vh).astype(BF16))
    o = jnp.concatenate(heads, axis=1)
    o_ref[...] = x + _rms(_dot(o, wo_ref[...]), g_ref[1:2])


def _xattn(x2, gains, wq, kv, wo, *, tm, tiles_per_batch):
    n, d = x2.shape
    m = kv.shape[1]
    return pl.pallas_call(
        _xattn_kernel,
        grid=(n // tm,),
        in_specs=[
            pl.BlockSpec((tm, d), lambda i: (i, 0)),
            _const_spec((2, d)),
            _const_spec((d, d)),
            pl.BlockSpec((1, m, 2 * d), lambda i: (i // tiles_per_batch, 0, 0)),
            _const_spec((d, d)),
        ],
        out_specs=pl.BlockSpec((tm, d), lambda i: (i, 0)),
        out_shape=jax.ShapeDtypeStruct((n, d), F32),
        compiler_params=_params(),
        name="xattn",
    )(x2, gains, wq, kv, wo)


def _ffn_kernel(x_ref, g_ref, win_ref, wout_ref, o_ref, *, d_ff, n_split):
    x = x_ref[...]
    h = _rms(x, g_ref[0:1]).astype(BF16)
    ck = d_ff // n_split
    acc = None
    for c in range(n_split):
        gate = _dot(h, win_ref[:, c * ck:(c + 1) * ck])
        up = _dot(h, win_ref[:, d_ff + c * ck:d_ff + (c + 1) * ck])
        act = (gate * _sigmoid(gate) * up).astype(BF16)
        part = _dot(act, wout_ref[c * ck:(c + 1) * ck, :])
        acc = part if acc is None else acc + part
    o_ref[...] = x + _rms(acc, g_ref[1:2])


def _ffn(x2, gains, w_in, w_out, *, tm):
    n, d = x2.shape
    d_ff = w_out.shape[0]
    n_split = 2 if d_ff % (2 * LANES) == 0 else 1
    return pl.pallas_call(
        functools.partial(_ffn_kernel, d_ff=d_ff, n_split=n_split),
        grid=(n // tm,),
        in_specs=[
            pl.BlockSpec((tm, d), lambda i: (i, 0)),
            _const_spec((2, d)),
            _const_spec((d, 2 * d_ff)),
            _const_spec((d_ff, d)),
        ],
        out_specs=pl.BlockSpec((tm, d), lambda i: (i, 0)),
        out_shape=jax.ShapeDtypeStruct((n, d), F32),
        compiler_params=_params(),
        name="ffn",
    )(x2, gains, w_in, w_out)


def _gmlp_kernel(x_ref, g_ref, win_ref, bin_ref, ln_ref, ws_ref, bs_ref, wout_ref, o_ref, mix_ref, *, tm):
    x = x_ref[...]
    h = _rms(x, g_ref[0:1]).astype(BF16)
    hh = _dot(h, win_ref[...]) + bin_ref[...]
    hh = 0.5 * hh * (1.0 + lax.erf(hh * (2.0 ** -0.5)))
    u = hh[:, :D_MODEL]
    v = hh[:, D_MODEL:]
    mu = jnp.mean(v, axis=-1, keepdims=True)
    vc = v - mu
    var = jnp.mean(vc * vc, axis=-1, keepdims=True)
    vn = (vc * lax.rsqrt(var + LN_EPS) * ln_ref[0:1] + ln_ref[1:2]).astype(BF16)
    for c in range(tm // GM_CHUNK):
        r0 = c * GM_CHUNK
        for g in range(GM_GROUPS):
            c0 = g * GM_GROUP_DIM
            mix_ref[r0:r0 + GM_CHUNK, c0:c0 + GM_GROUP_DIM] = (
                _dot(ws_ref[g], vn[r0:r0 + GM_CHUNK, c0:c0 + GM_GROUP_DIM]) + bs_ref[:, c0:c0 + GM_GROUP_DIM])
    gated = (u * mix_ref[...]).astype(BF16)
    o_ref[...] = x + _rms(_dot(gated, wout_ref[...]), g_ref[1:2])


def _gmlp(x2, gains, w_in, b_in, ln, w_s, b_s, w_out, *, tm):
    n, d = x2.shape
    return pl.pallas_call(
        functools.partial(_gmlp_kernel, tm=tm),
        grid=(n // tm,),
        in_specs=[
            pl.BlockSpec((tm, d), lambda i: (i, 0)),
            _const_spec((2, d)),
            _const_spec((d, 2 * d)),
            _const_spec((1, 2 * d)),
            _const_spec((2, d)),
            _const_spec((GM_GROUPS, GM_CHUNK, GM_CHUNK)),
            _const_spec((GM_CHUNK, d)),
            _const_spec((d, d)),
        ],
        out_specs=pl.BlockSpec((tm, d), lambda i: (i, 0)),
        out_shape=jax.ShapeDtypeStruct((n, d), F32),
        scratch_shapes=[pltpu.VMEM((tm, d), F32)],
        compiler_params=_params(),
        name="gmlp",
    )(x2, gains, w_in, b_in, ln, w_s, b_s, w_out)


def _head_sum(z, e_ref, et_ref):
    s = jnp.dot(z, e_ref[...], preferred_element_type=F32, precision=HI)
    return jnp.dot(s, et_ref[...], preferred_element_type=F32, precision=HI)


def _rwkv_pre_kernel(*refs, tm, tiles_per_batch, has_vres):
    if has_vres:
        (x_ref, xp_ref, g_ref, mix_ref, wrkv_ref, vec_ref, w1_ref, w2_ref, a1_ref, a2_ref, g1_ref, g2_ref,
         e_ref, et_ref, vf_ref, v1_ref, v2_ref,
         r_ref, lw_ref, k_ref, v_ref, an_ref, b_ref, gate_ref) = refs
    else:
        (x_ref, xp_ref, g_ref, mix_ref, wrkv_ref, vec_ref, w1_ref, w2_ref, a1_ref, a2_ref, g1_ref, g2_ref,
         e_ref, et_ref,
         r_ref, lw_ref, k_ref, v_ref, an_ref, b_ref, gate_ref) = refs
    i = pl.program_id(0)
    gain = g_ref[0:1]
    h = _rms(x_ref[...], gain)
    hp = _rms(xp_ref[SUBLANES - 1:SUBLANES, :], gain)
    hp = jnp.where(i % tiles_per_batch == 0, 0.0, hp)
    row = lax.broadcasted_iota(jnp.int32, (tm, 1), 0)
    shifted = jnp.where(row == 0, hp, pltpu.roll(h, 1, axis=0))
    xx = shifted - h

    def mixed(p):
        return (h + xx * mix_ref[p:p + 1]).astype(BF16)

    w0, a0, k_k, k_a = vec_ref[0:1], vec_ref[1:2], vec_ref[2:3], vec_ref[3:4]
    r = _dot(mixed(0), wrkv_ref[0])
    k = _dot(mixed(1), wrkv_ref[1])
    xv = mixed(2)
    v = _dot(xv, wrkv_ref[2])
    z = w0 + _dot(jnp.tanh(_dot(mixed(3), w1_ref[...])).astype(BF16), w2_ref[...])
    w = -(jnp.maximum(-z, 0.0) + jnp.log(1.0 + jnp.exp(-jnp.abs(z)))) - 0.5
    a = _sigmoid(a0 + _dot(_dot(mixed(4), a1_ref[...]).astype(BF16), a2_ref[...]))
    gate = _dot(_sigmoid(_dot(mixed(5), g1_ref[...])).astype(BF16), g2_ref[...])
    if has_vres:
        v0 = vec_ref[4:5]
        sv = _sigmoid(v0 + _dot(_dot(xv, v1_ref[...]).astype(BF16), v2_ref[...]))
        v = v + (vf_ref[...] - v) * sv
    kk = k * k_k
    nrm = jnp.sqrt(_head_sum(kk * kk, e_ref, et_ref))
    kk = kk / jnp.maximum(nrm, NORM_EPS)
    r_ref[...] = r
    lw_ref[...] = -jnp.exp(w)
    k_ref[...] = k * (1.0 + (a - 1.0) * k_a)
    v_ref[...] = v
    an_ref[...] = -kk
    b_ref[...] = kk * a
    gate_ref[...] = gate


def _rwkv_pre(x2, gain, mix, wrkv, vecs, w1, w2, a1, a2, g1, g2, e, et, vres, *, tm, tiles_per_batch):
    n, d = x2.shape
    has_vres = vres is not None
    tok = pl.BlockSpec((tm, d), lambda i: (i, 0))
    prev = pl.BlockSpec((SUBLANES, d), lambda i: (jnp.maximum(i * (tm // SUBLANES) - 1, 0), 0))
    in_specs = [tok, prev, _const_spec(gain.shape), _const_spec(mix.shape), _const_spec(wrkv.shape),
                _const_spec(vecs.shape), _const_spec(w1.shape), _const_spec(w2.shape), _const_spec(a1.shape),
                _const_spec(a2.shape), _const_spec(g1.shape), _const_spec(g2.shape), _const_spec(e.shape),
                _const_spec(et.shape)]
    args = [x2, x2, gain, mix, wrkv, vecs, w1, w2, a1, a2, g1, g2, e, et]
    if has_vres:
        v_first, v1, v2 = vres
        in_specs += [tok, _const_spec(v1.shape), _const_spec(v2.shape)]
        args += [v_first, v1, v2]
    return pl.pallas_call(
        functools.partial(_rwkv_pre_kernel, tm=tm, tiles_per_batch=tiles_per_batch, has_vres=has_vres),
        grid=(n // tm,),
        in_specs=in_specs,
        out_specs=[tok] * 7,
        out_shape=[jax.ShapeDtypeStruct((n, d), F32)] * 7,
        compiler_params=_params(),
        name="rwkv_pre",
    )(*args)


def _rwkv_scan_kernel(r_ref, lw_ref, k_ref, v_ref, an_ref, b_ref, y_ref, s_ref, *, n_sub):
    L, P = RW_CHUNK, RW_STACK

    @pl.when(pl.program_id(1) == 0)
    def _():
        s_ref[...] = jnp.zeros_like(s_ref)

    ri = lax.broadcasted_iota(jnp.int32, (P, P), 0)
    ci = lax.broadcasted_iota(jnp.int32, (P, P), 1)
    same = (ri // L) == (ci // L)
    mask_strict = same & ((ri % L) > (ci % L))
    mask_incl = same & ((ri % L) >= (ci % L))
    eye = (ri == ci).astype(F32)
    level_masks = []
    s = 1
    while s < L:
        level_masks.append(((ri // (2 * s)) == (ci // (2 * s))) & ((ri // s) > (ci // s)))
        s *= 2
    ti =lax.broadcasted_iota(jnp.int32, (L, L), 0)
    tj = lax.broadcasted_iota(jnp.int32, (L, L), 1)
    tri = (ti >= tj).astype(F32)
    head0 = lax.broadcasted_iota(jnp.int32, (1, LANES), 1) < RW_HEAD_DIM

    def stack(z):
        return jnp.concatenate([jnp.where(head0, z, 0.0), jnp.where(head0, 0.0, z)], axis=0)

    def chunk(j, carry):
        rows = pl.ds(pl.multiple_of(j * L, L), L)
        lw_all = lw_ref[0, rows, :]
        cum_all = jnp.dot(tri, lw_all, preferred_element_type=F32, precision=HI)
        for g in range(RW_GROUPS):
            lanes = slice(g * LANES, (g + 1) * LANES)
            lw = lw_all[:, lanes]
            cum = cum_all[:, lanes]
            w_in = jnp.exp(cum)
            w_ex = jnp.exp(cum - lw)
            w_inv = jnp.exp(-cum)
            v = v_ref[0, rows, lanes]
            x_a = stack(an_ref[0, rows, lanes] * w_ex)
            x_r = stack(r_ref[0, rows, lanes] * w_in)
            v_s = stack(v).astype(BF16)
            bt = (b_ref[0, rows, lanes] * w_inv).astype(BF16)
            kt = (k_ref[0, rows, lanes] * w_inv).astype(BF16)
            y_b = jnp.concatenate([bt, bt], axis=0)
            y_k = jnp.concatenate([kt, kt], axis=0)
            xx = jnp.concatenate([x_a, x_r], axis=0).astype(BF16)
            yy = jnp.concatenate([y_b, y_k], axis=0)
            aa = _dot_nt(xx, yy)
            a_ab = jnp.where(mask_strict, aa[:P, :P], 0.0)
            a_ak = jnp.where(mask_strict, aa[:P, P:], 0.0).astype(BF16)
            a_rb = jnp.where(mask_incl, aa[P:, :P], 0.0).astype(BF16)
            a_rk = jnp.where(mask_incl, aa[P:, P:], 0.0).astype(BF16)
            tm = eye + jnp.where(level_masks[0], a_ab, 0.0)
            for lm in level_masks[1:]:
                tb = tm.astype(BF16)
                tm = tm + _dot(_dot(tb, jnp.where(lm, a_ab, 0.0).astype(BF16)).astype(BF16), tb)
            s_bd = s_ref[g]
            s_b = s_bd.astype(BF16)
            z = _dot_nt(xx[:P], s_b) + _dot(a_ak, v_s)
            u_s = _dot(tm.astype(BF16), z.astype(BF16)).astype(BF16)
            y_s = _dot_nt(xx[P:], s_b) + _dot(a_rb, u_s) + _dot(a_rk, v_s)
            y_ref[0, rows, lanes] = y_s[:L] + y_s[L:]
            ds = _dot_tn(u_s, y_b) + _dot_tn(v_s, y_k)
            s_ref[g] = (s_bd + jnp.where(same, ds, 0.0)) * w_in[L - 1:L, :]
        return carry

    lax.fori_loop(0, n_sub, chunk, 0)


def _rwkv_scan(r, lw, k, v, an, b, *, n_sub):
    bsz, t, d = r.shape
    lb = n_sub * RW_CHUNK
    spec = pl.BlockSpec((1, lb, d), lambda i, j: (i, j, 0))
    return pl.pallas_call(
        functools.partial(_rwkv_scan_kernel, n_sub=n_sub),
        grid=(bsz, t // lb),
        in_specs=[spec] * 6,
        out_specs=spec,
        out_shape=jax.ShapeDtypeStruct((bsz, t, d), F32),
        scratch_shapes=[pltpu.VMEM((RW_GROUPS, LANES, LANES), F32)],
        compiler_params=_params(2),
        name="rwkv_scan",
    )(r, lw, k, v, an, b)


def _rwkv_post_kernel(x_ref, y_ref, r_ref, k_ref, v_ref, gate_ref, g_ref, vec_ref, e_ref, et_ref, wo_ref, o_ref):
    y = y_ref[...]
    inv_n = 1.0 / RW_HEAD_DIM
    mu = _head_sum(y, e_ref, et_ref) * inv_n
    yc = y - mu
    var = _head_sum(yc * yc, e_ref, et_ref) * inv_n
    yn = yc * lax.rsqrt(var + RW_GN_EPS) * vec_ref[0:1] + vec_ref[1:2]
    bonus = _head_sum(r_ref[...] * k_ref[...] * vec_ref[2:3], e_ref, et_ref)
    yn = yn + bonus * v_ref[...]
    out = _dot((yn * gate_ref[...]).astype(BF16), wo_ref[...])
    o_ref[...] = x_ref[...] + _rms(out, g_ref[0:1])


def _rwkv_post(x2, y, r, k, v, gate, gain, vecs, e, et, wo, *, tm):
    n, d = x2.shape
    tok = pl.BlockSpec((tm, d), lambda i: (i, 0))
    return pl.pallas_call(
        _rwkv_post_kernel,
        grid=(n // tm,),
        in_specs=[tok] * 6 + [_const_spec(gain.shape), _const_spec(vecs.shape), _const_spec(e.shape),
                              _const_spec(et.shape), _const_spec(wo.shape)],
        out_specs=tok,
        out_shape=jax.ShapeDtypeStruct((n, d), F32),
        compiler_params=_params(),
        name="rwkv_post",
    )(x2, y, r, k, v, gate, gain, vecs, e, et, wo)


def kernel(x, mem, norm_gains, mem_norm_gains, xa_wq, xa_wkv, xa_wo, ffn_w_in, ffn_w_out, gm_w_in, gm_b_in, gm_ln_g, gm_ln_b, gm_w_s, gm_b_s, gm_w_out, rw_mix, rw_w_rkv, rw_w0, rw_w1, rw_w2, rw_a0, rw_a1, rw_a2, rw_g1, rw_g2, rw_k_k, rw_k_a, rw_r_k, rw_ln_g, rw_ln_b, rw_w_o, rw_v0, rw_v1, rw_v2):
    bsz, seq, d = x.shape
    depth = norm_gains.shape[0]
    assert d == D_MODEL
    tm = min(512, seq)
    assert seq % tm == 0 and tm % GM_CHUNK == 0
    tiles_per_batch = seq // tm
    tm_pre = min(256, seq)
    assert seq % tm_pre == 0
    n_sub = min(4, seq // RW_CHUNK)
    assert seq % (n_sub * RW_CHUNK) == 0

    bf = lambda w: w.astype(BF16)
    kv_all = _memkv(mem, mem_norm_gains, bf(xa_wkv))
    e = (jnp.arange(d)[:, None] // RW_HEAD_DIM == jnp.arange(LANES)[None, :]).astype(F32)
    et = e.T

    x2 = x.reshape(bsz * seq, d)
    v_first = None
    for i in range(depth):
        g = norm_gains[i]
        j = i // 2
        if i % 2 == 0:
            x2 = _gmlp(x2, g[0:2], bf(gm_w_in[j]), gm_b_in[j][None, :],
                       jnp.stack([gm_ln_g[j], gm_ln_b[j]]), bf(jnp.tril(gm_w_s[j])),
                       jnp.repeat(gm_b_s[j].T, GM_GROUP_DIM, axis=1), bf(gm_w_out[j]), tm=tm)
        else:
            has_vres = j > 0
            vec_rows = [rw_w0[j], rw_a0[j], rw_k_k[j], rw_k_a[j]]
            vres = None
            if has_vres:
                vec_rows.append(rw_v0[j - 1])
                vres = (v_first, bf(rw_v1[j - 1]), bf(rw_v2[j - 1]))
            r, lw, k, v, an, b, gate = _rwkv_pre(
                x2, g[0:1], rw_mix[j], bf(rw_w_rkv[j]), jnp.stack(vec_rows), bf(rw_w1[j]), bf(rw_w2[j]),
                bf(rw_a1[j]), bf(rw_a2[j]), bf(rw_g1[j]), bf(rw_g2[j]), e, et, vres,
                tm=tm_pre, tiles_per_batch=seq // tm_pre)
            if not has_vres:
                v_first = v
            sh = (bsz, seq, d)
            y = _rwkv_scan(r.reshape(sh), lw.reshape(sh), k.reshape(sh), v.reshape(sh), an.reshape(sh),
                           b.reshape(sh), n_sub=n_sub)
            x2 = _rwkv_post(x2, y.reshape(bsz * seq, d), r, k, v, gate, g[1:2],
                            jnp.stack([rw_ln_g[j], rw_ln_b[j], rw_r_k[j].reshape(d)]), e, et, bf(rw_w_o[j]), tm=tm)
        x2 = _xattn(x2, g[2:4], bf(xa_wq[i]), kv_all[i], bf(xa_wo[i]), tm=tm, tiles_per_batch=tiles_per_batch)
        x2 = _ffn(x2, g[4:6], bf(ffn_w_in[i]), bf(ffn_w_out[i]), tm=tm)
    return x2.reshape(bsz, seq, d)
```

```python
import functools

import jax
import jax.numpy as jnp
from jax import lax
from jax.experimental import pallas as pl
from jax.experimental.pallas import tpu as pltpu

F32 = jnp.float32
BF16 = jnp.bfloat16

D_MODEL = 1024
XA_HEADS = 4
XA_HEAD_DIM = D_MODEL // XA_HEADS
GM_CHUNK = 128
GM_GROUPS = 8
GM_GROUP_DIM = D_MODEL // GM_GROUPS
RW_HEAD_DIM = 64
RW_HEADS = D_MODEL // RW_HEAD_DIM
RW_GN_EPS = 64e-5
NORM_EPS = 1e-12
RMS_EPS = 1e-6
LN_EPS = 1e-5

LANES = 128
SUBLANES = 8
RW_CHUNK = 64
RW_PAIR = LANES // RW_HEAD_DIM
RW_GROUPS = D_MODEL // LANES
RW_STACK = RW_PAIR * RW_CHUNK
VMEM_LIMIT = 56 * 1024 * 1024

HI = lax.Precision.HIGHEST


def _dot(a, b):
    return jnp.dot(a, b, preferred_element_type=F32)


def _dot_nt(a, b):
    return lax.dot_general(a, b, (((1,), (1,)), ((), ())), preferred_element_type=F32)


def _dot_tn(a, b):
    return lax.dot_general(a, b, (((0,), (0,)), ((), ())), preferred_element_type=F32)


def _rms(x, g):
    return x * lax.rsqrt(jnp.mean(x * x, axis=-1, keepdims=True) + RMS_EPS) * g


def _sigmoid(x):
    return 1.0 / (1.0 + jnp.exp(-x))


def _const_spec(shape):
    nd = len(shape)
    return pl.BlockSpec(shape, lambda *_: (0,) * nd, pipeline_mode=pl.Buffered(1))


def _params(n_axes=1):
    return pltpu.CompilerParams(dimension_semantics=("arbitrary",) * n_axes,
                                vmem_limit_bytes=VMEM_LIMIT)


def _memkv_kernel(mem_ref, g_ref, w_ref, o_ref):
    m = _rms(mem_ref[0], g_ref[0]).astype(BF16)
    o_ref[0, 0] = _dot(m, w_ref[0]).astype(BF16)


def _memkv(mem, gains, wkv):
    depth = wkv.shape[0]
    b, m, d = mem.shape
    return pl.pallas_call(
        _memkv_kernel,
        grid=(depth, b),
        in_specs=[
            pl.BlockSpec((1, m, d), lambda i, j: (j, 0, 0)),
            pl.BlockSpec((1, 1, d), lambda i, j: (i, 0, 0)),
            pl.BlockSpec((1, d, 2 * d), lambda i, j: (i, 0, 0)),
        ],
        out_specs=pl.BlockSpec((1, 1, m, 2 * d), lambda i, j: (i, j, 0, 0)),
        out_shape=jax.ShapeDtypeStruct((depth, b, m, 2 * d), BF16),
        compiler_params=_params(2),
        name="memkv",
    )(mem, gains.reshape(depth, 1, d), wkv)


def _xattn_kernel(x_ref, g_ref, wq_ref, kv_ref, wo_ref, o_ref):
    x = x_ref[...]
    h = _rms(x, g_ref[0:1]).astype(BF16)
    q = _dot(h, wq_ref[...]) * (XA_HEAD_DIM ** -0.5)
    heads = []
    for hd in range(XA_HEADS):
        lo = hd * XA_HEAD_DIM
        qh = q[:, lo:lo + XA_HEAD_DIM].astype(BF16)
        kh = kv_ref[0, :, lo:lo + XA_HEAD_DIM]
        vh = kv_ref[0, :, D_MODEL + lo:D_MODEL + lo + XA_HEAD_DIM]
        s = _dot_nt(qh, kh)
        p = jnp.exp(s - jnp.max(s, axis=-1, keepdims=True))
        p = p / jnp.sum(p, axis=-1, keepdims=True)
        heads.append(_dot(p.astype(BF16), vh).astype(BF16))
    o = jnp.concatenate(heads, axis=1)
    o_ref[...] = x + _rms(_dot(o, wo_ref[...]), g_ref[1:2])


def _xattn(x2, gains, wq, kv, wo, *, tm, tiles_per_batch):
    n, d = x2.shape
    m = kv.shape[1]
    return pl.pallas_call(
        _xattn_kernel,
        grid=(n // tm,),
        in_specs=[
            pl.BlockSpec((tm, d), lambda i: (i, 0)),
            _const_spec((2, d)),
            _const_spec((d, d)),
            pl.BlockSpec((1, m, 2 * d), lambda i: (i // tiles_per_batch, 0, 0)),
            _const_spec((d, d)),
        ],
        out_specs=pl.BlockSpec((tm, d), lambda i: (i, 0)),
        out_shape=jax.ShapeDtypeStruct((n, d), F32),
        compiler_params=_params(),
        name="xattn",
    )(x2, gains, wq, kv, wo)


def _ffn_kernel(x_ref, g_ref, win_ref, wout_ref, o_ref, *, d_ff, n_split):
    x = x_ref[...]
    h = _rms(x, g_ref[0:1]).astype(BF16)
    ck = d_ff // n_split
    acc = None
    for c in range(n_split):
        gate = _dot(h, win_ref[:, c * ck:(c + 1) * ck])
        up = _dot(h, win_ref[:, d_ff + c * ck:d_ff + (c + 1) * ck])
        act = (gate * _sigmoid(gate) * up).astype(BF16)
        part = _dot(act, wout_ref[c * ck:(c + 1) * ck, :])
        acc = part if acc is None else acc + part
    o_ref[...] = x + _rms(acc, g_ref[1:2])


def _ffn(x2, gains, w_in, w_out, *, tm):
    n, d = x2.shape
    d_ff = w_out.shape[0]
    n_split = 2 if d_ff % (2 * LANES) == 0 else 1
    return pl.pallas_call(
        functools.partial(_ffn_kernel, d_ff=d_ff, n_split=n_split),
        grid=(n // tm,),
        in_specs=[
            pl.BlockSpec((tm, d), lambda i: (i, 0)),
            _const_spec((2, d)),
            _const_spec((d, 2 * d_ff)),
            _const_spec((d_ff, d)),
        ],
        out_specs=pl.BlockSpec((tm, d), lambda i: (i, 0)),
        out_shape=jax.ShapeDtypeStruct((n, d), F32),
        compiler_params=_params(),
        name="ffn",
    )(x2, gains, w_in, w_out)


def _gmlp_kernel(x_ref, g_ref, win_ref, bin_ref, ln_ref, ws_ref, bs_ref, wout_ref, o_ref, mix_ref, *, tm):
    x = x_ref[...]
    h = _rms(x, g_ref[0:1]).astype(BF16)
    hh = _dot(h, win_ref[...]) + bin_ref[...]
    hh = 0.5 * hh * (1.0 + lax.erf(hh * (2.0 ** -0.5)))
    u = hh[:, :D_MODEL]
    v = hh[:, D_MODEL:]
    mu = jnp.mean(v, axis=-1, keepdims=True)
    vc = v - mu
    var = jnp.mean(vc * vc, axis=-1, keepdims=True)
    vn = (vc * lax.rsqrt(var + LN_EPS) * ln_ref[0:1] + ln_ref[1:2]).astype(BF16)
    for c in range(tm // GM_CHUNK):
        r0 = c * GM_CHUNK
        for g in range(GM_GROUPS):
            c0 = g * GM_GROUP_DIM
            mix_ref[r0:r0 + GM_CHUNK, c0:c0 + GM_GROUP_DIM] = (
                _dot(ws_ref[g], vn[r0:r0 + GM_CHUNK, c0:c0 + GM_GROUP_DIM]) + bs_ref[:, c0:c0 + GM_GROUP_DIM])
    gated = (u * mix_ref[...]).astype(BF16)
    o_ref[...] = x + _rms(_dot(gated, wout_ref[...]), g_ref[1:2])


def _gmlp(x2, gains, w_in, b_in, ln, w_s, b_s, w_out, *, tm):
    n, d = x2.shape
    return pl.pallas_call(
        functools.partial(_gmlp_kernel, tm=tm),
        grid=(n // tm,),
        in_specs=[
            pl.BlockSpec((tm, d), lambda i: (i, 0)),
            _const_spec((2, d)),
            _const_spec((d, 2 * d)),
            _const_spec((1, 2 * d)),
            _const_spec((2, d)),
            _const_spec((GM_GROUPS, GM_CHUNK, GM_CHUNK)),
            _const_spec((GM_CHUNK, d)),
            _const_spec((d, d)),
        ],
        out_specs=pl.BlockSpec((tm, d), lambda i: (i, 0)),
        out_shape=jax.ShapeDtypeStruct((n, d), F32),
        scratch_shapes=[pltpu.VMEM((tm, d), F32)],
        compiler_params=_params(),
        name="gmlp",
    )(x2, gains, w_in, b_in, ln, w_s, b_s, w_out)


def _rwkv_pre_kernel(*refs, tm, tiles_per_batch, has_vres):
    if has_vres:
        (x_ref, xp_ref, g_ref, mix_ref, wrkv_ref, vec_ref, w1_ref, w2_ref, a1_ref, a2_ref, g1_ref, g2_ref,
         vf_ref, v1_ref, v2_ref,
         r_ref, lw_ref, k_ref, v_ref, kk_ref, a_ref, gate_ref) = refs
    else:
        (x_ref, xp_ref, g_ref, mix_ref, wrkv_ref, vec_ref, w1_ref, w2_ref, a1_ref, a2_ref, g1_ref, g2_ref,
         r_ref, lw_ref, k_ref, v_ref, kk_ref, a_ref, gate_ref) = refs
    i = pl.program_id(0)
    gain = g_ref[0:1]
    h = _rms(x_ref[...], gain)
    hp = _rms(xp_ref[SUBLANES - 1:SUBLANES, :], gain)
    hp = jnp.where(i % tiles_per_batch == 0, 0.0, hp)
    row = lax.broadcasted_iota(jnp.int32, (tm, 1), 0)
    shifted = jnp.where(row == 0, hp, pltpu.roll(h, 1, axis=0))
    xx = shifted - h

    def mixed(p):
        return (h + xx * mix_ref[p:p + 1]).astype(BF16)

    w0, a0, k_k, k_a = vec_ref[0:1], vec_ref[1:2], vec_ref[2:3], vec_ref[3:4]
    r = _dot(mixed(0), wrkv_ref[0])
    k = _dot(mixed(1), wrkv_ref[1])
    xv = mixed(2)
    v = _dot(xv, wrkv_ref[2])
    z = w0 + _dot(jnp.tanh(_dot(mixed(3), w1_ref[...])).astype(BF16), w2_ref[...])
    w = -(jnp.maximum(-z, 0.0) + jnp.log(1.0 + jnp.exp(-jnp.abs(z)))) - 0.5
    a = _sigmoid(a0 + _dot(_dot(mixed(4), a1_ref[...]).astype(BF16), a2_ref[...]))
    gate = _dot(_sigmoid(_dot(mixed(5), g1_ref[...])).astype(BF16), g2_ref[...])
    if has_vres:
        v0 = vec_ref[4:5]
        sv = _sigmoid(v0 + _dot(_dot(xv, v1_ref[...]).astype(BF16), v2_ref[...]))
        v = v + (vf_ref[...] - v) * sv
    r_ref[...] = r
    lw_ref[...] = -jnp.exp(w)
    k_ref[...] = k * (1.0 + (a - 1.0) * k_a)
    v_ref[...] = v
    kk_ref[...] = k * k_k
    a_ref[...] = a
    gate_ref[...] = gate


def _rwkv_pre(x2, gain, mix, wrkv, vecs, w1, w2, a1, a2, g1, g2, vres, *, tm, tiles_per_batch):
    n, d = x2.shape
    has_vres = vres is not None
    tok = pl.BlockSpec((tm, d), lambda i: (i, 0))
    prev = pl.BlockSpec((SUBLANES, d), lambda i: (jnp.maximum(i * (tm // SUBLANES) - 1, 0), 0))
    in_specs = [tok, prev, _const_spec(gain.shape), _const_spec(mix.shape), _const_spec(wrkv.shape),
                _const_spec(vecs.shape), _const_spec(w1.shape), _const_spec(w2.shape), _const_spec(a1.shape),
                _const_spec(a2.shape), _const_spec(g1.shape), _const_spec(g2.shape)]
    args = [x2, x2, gain, mix, wrkv, vecs, w1, w2, a1, a2, g1, g2]
    if has_vres:
        v_first, v1, v2 = vres
        in_specs += [tok, _const_spec(v1.shape), _const_spec(v2.shape)]
        args += [v_first, v1, v2]
    return pl.pallas_call(
        functools.partial(_rwkv_pre_kernel, tm=tm, tiles_per_batch=tiles_per_batch, has_vres=has_vres),
        grid=(n // tm,),
        in_specs=in_specs,
        out_specs=[tok] * 7,
        out_shape=[jax.ShapeDtypeStruct((n, d), F32)] * 7,
        compiler_params=_params(),
        name="rwkv_pre",
    )(*args)


def _rwkv_scan_kernel(r_ref, lw_ref, k_ref, v_ref, kk_ref, a_ref, gate_ref, vec_ref, y_ref, s_ref, *, n_sub):
    L, P = RW_CHUNK, RW_STACK

    @pl.when(pl.program_id(1) == 0)
    def _():
        s_ref[...] = jnp.zeros_like(s_ref)

    ri = lax.broadcasted_iota(jnp.int32, (P, P), 0)
    ci = lax.broadcasted_iota(jnp.int32, (P, P), 1)
    same = (ri // L) == (ci // L)
    mask_strict = same & ((ri % L) > (ci % L))
    mask_incl = same & ((ri % L) >= (ci % L))
    eye = (ri == ci).astype(F32)
    level_masks = []
    s = 1
    while s < L:
        level_masks.append(((ri // (2 * s)) == (ci // (2 * s))) & ((ri // s) > (ci // s)))
        s *= 2
    ti =lax.broadcasted_iota(jnp.int32, (L, L), 0)
    tj = lax.broadcasted_iota(jnp.int32, (L, L), 1)
    tri = (ti >= tj).astype(F32)
    head0 = lax.broadcasted_iota(jnp.int32, (1, LANES), 1) < RW_HEAD_DIM

    def stack(z):
        return jnp.concatenate([jnp.where(head0, z, 0.0), jnp.where(head0, 0.0, z)], axis=0)

    def head_sum(z):
        s0 = jnp.sum(jnp.where(head0, z, 0.0), axis=-1, keepdims=True)
        s1 = jnp.sum(jnp.where(head0, 0.0, z), axis=-1, keepdims=True)
        return jnp.where(head0, s0, s1)

    def chunk(j, carry):
        rows = pl.ds(pl.multiple_of(j * L, L), L)
        lw_all = lw_ref[0, rows, :]
        cum_all = jnp.dot(tri, lw_all, preferred_element_type=F32, precision=HI)
        groups = range(RW_GROUPS)
        lanes = [slice(g * LANES, (g + 1) * LANES) for g in groups]
        cum = [cum_all[:, ln] for ln in lanes]
        w_in = [jnp.exp(c) for c in cum]
        w_ex = [jnp.exp(c - lw_all[:, ln]) for c, ln in zip(cum, lanes)]
        w_inv = [jnp.exp(-c) for c in cum]
        kk = [kk_ref[0, rows, ln] for ln in lanes]
        kk = [z / jnp.maximum(jnp.sqrt(head_sum(z * z)), NORM_EPS) for z in kk]
        x_a = [stack(-z * w).astype(BF16) for z, w in zip(kk, w_ex)]
        x_r = [stack(r_ref[0, rows, ln] * w).astype(BF16) for ln, w in zip(lanes, w_in)]
        v_s = [stack(v_ref[0, rows, ln]).astype(BF16) for ln in lanes]
        bt = [(z * a_ref[0, rows, ln] * w).astype(BF16) for z, ln, w in zip(kk, lanes, w_inv)]
        kt = [(k_ref[0, rows, ln] * w).astype(BF16) for ln, w in zip(lanes, w_inv)]
        y_b = [jnp.concatenate([z, z], axis=0) for z in bt]
        y_k = [jnp.concatenate([z, z], axis=0) for z in kt]
        aa = [_dot_nt(jnp.concatenate([xa, xr], axis=0), jnp.concatenate([yb, yk], axis=0))
              for xa, xr, yb, yk in zip(x_a, x_r, y_b, y_k)]
        a_ab = [jnp.where(mask_strict, z[:P, :P], 0.0) for z in aa]
        a_ak = [jnp.where(mask_strict, z[:P, P:], 0.0).astype(BF16) for z in aa]
        a_rb = [jnp.where(mask_incl, z[P:, :P], 0.0).astype(BF16) for z in aa]
        a_rk = [jnp.where(mask_incl, z[P:, P:], 0.0).astype(BF16) for z in aa]
        tm = [eye + jnp.where(level_masks[0], a, 0.0) for a in a_ab]
        for lm in level_masks[1:]:
            tb = [t.astype(BF16) for t in tm]
            ta = [_dot(t, jnp.where(lm, a, 0.0).astype(BF16)).astype(BF16) for t, a in zip(tb, a_ab)]
            tm = [t + _dot(m, t2) for t, m, t2 in zip(tm, ta, tb)]
        s_bd = [s_ref[g] for g in groups]
        s_b = [s.astype(BF16) for s in s_bd]
        z = [(_dot_nt(xa, s) + _dot(a, v)).astype(BF16) for xa, s, a, v in zip(x_a, s_b, a_ak, v_s)]
        u_s = [_dot(t.astype(BF16), zz).astype(BF16) for t, zz in zip(tm, z)]
        y_s = [_dot_nt(xr, s) + _dot(arb, u) + _dot(ark, v)
               for xr, s, arb, u, ark, v in zip(x_r, s_b, a_rb, u_s, a_rk, v_s)]
        ds = [_dot_tn(u, yb) + _dot_tn(v, yk) for u, yb, v, yk in zip(u_s, y_b, v_s, y_k)]
        inv_n = 1.0 / RW_HEAD_DIM
        for g in groups:
            ln = lanes[g]
            s_ref[g] = (s_bd[g] + jnp.where(same, ds[g], 0.0)) * w_in[g][L - 1:L, :]
            y = y_s[g][:L] + y_s[g][L:]
            yc = y - head_sum(y) * inv_n
            var = head_sum(yc * yc) * inv_n
            yn = yc * lax.rsqrt(var + RW_GN_EPS) * vec_ref[0:1, ln] + vec_ref[1:2, ln]
            bonus = head_sum(r_ref[0, rows, ln] * k_ref[0, rows, ln] * vec_ref[2:3, ln])
            out = (yn + bonus * v_ref[0, rows, ln]) * gate_ref[0, rows, ln]
            y_ref[0, rows, ln] = out.astype(BF16)
        return carry

    lax.fori_loop(0, n_sub, chunk, 0)


def _rwkv_scan(r, lw, k, v, kk, a, gate, vecs, *, n_sub):
    bsz, t, d = r.shape
    lb = n_sub * RW_CHUNK
    spec = pl.BlockSpec((1, lb, d), lambda i, j: (i, j, 0))
    return pl.pallas_call(
        functools.partial(_rwkv_scan_kernel, n_sub=n_sub),
        grid=(bsz, t // lb),
        in_specs=[spec] * 7 + [_const_spec(vecs.shape)],
        out_specs=spec,
        out_shape=jax.ShapeDtypeStruct((bsz, t, d), BF16),
        scratch_shapes=[pltpu.VMEM((RW_GROUPS, LANES, LANES), F32)],
        compiler_params=_params(2),
        name="rwkv_scan",
    )(r, lw, k, v, kk, a, gate, vecs)


def _rwkv_post_kernel(x_ref, y_ref, g_ref, wo_ref, o_ref):
    o_ref[...] = x_ref[...] + _rms(_dot(y_ref[...], wo_ref[...]), g_ref[0:1])


def _rwkv_post(x2, y, gain, wo, *, tm):
    n, d = x2.shape
    tok = pl.BlockSpec((tm, d), lambda i: (i, 0))
    return pl.pallas_call(
        _rwkv_post_kernel,
        grid=(n // tm,),
        in_specs=[tok, tok, _const_spec(gain.shape), _const_spec(wo.shape)],
        out_specs=tok,
        out_shape=jax.ShapeDtypeStruct((n, d), F32),
        compiler_params=_params(),
        name="rwkv_post",
    )(x2, y, gain, wo)


def kernel(x, mem, norm_gains, mem_norm_gains, xa_wq, xa_wkv, xa_wo, ffn_w_in, ffn_w_out, gm_w_in, gm_b_in, gm_ln_g, gm_ln_b, gm_w_s, gm_b_s, gm_w_out, rw_mix, rw_w_rkv, rw_w0, rw_w1, rw_w2, rw_a0, rw_a1, rw_a2, rw_g1, rw_g2, rw_k_k, rw_k_a, rw_r_k, rw_ln_g, rw_ln_b, rw_w_o, rw_v0, rw_v1, rw_v2):
    bsz, seq, d = x.shape
    depth = norm_gains.shape[0]
    assert d == D_MODEL
    tm = min(512, seq)
    assert seq % tm == 0 and tm % GM_CHUNK == 0
    tiles_per_batch = seq // tm
    tm_pre = min(256, seq)
    assert seq % tm_pre == 0
    n_sub = min(4, seq // RW_CHUNK)
    assert seq % (n_sub * RW_CHUNK) == 0

    bf = lambda w: w.astype(BF16)
    kv_all = _memkv(mem, mem_norm_gains, bf(xa_wkv))

    x2 = x.reshape(bsz * seq, d)
    v_first = None
    for i in range(depth):
        g = norm_gains[i]
        j = i // 2
        if i % 2 == 0:
            x2 = _gmlp(x2, g[0:2], bf(gm_w_in[j]), gm_b_in[j][None, :],
                       jnp.stack([gm_ln_g[j], gm_ln_b[j]]), bf(jnp.tril(gm_w_s[j])),
                       jnp.repeat(gm_b_s[j].T, GM_GROUP_DIM, axis=1), bf(gm_w_out[j]), tm=tm)
        else:
            has_vres = j > 0
            vec_rows = [rw_w0[j], rw_a0[j], rw_k_k[j], rw_k_a[j]]
            vres = None
            if has_vres:
                vec_rows.append(rw_v0[j - 1])
                vres = (v_first, bf(rw_v1[j - 1]), bf(rw_v2[j - 1]))
            pre = _rwkv_pre(
                x2, g[0:1], rw_mix[j], bf(rw_w_rkv[j]), jnp.stack(vec_rows), bf(rw_w1[j]), bf(rw_w2[j]),
                bf(rw_a1[j]), bf(rw_a2[j]), bf(rw_g1[j]), bf(rw_g2[j]), vres,
                tm=tm_pre, tiles_per_batch=seq // tm_pre)
            if not has_vres:
                v_first = pre[3]
            y = _rwkv_scan(*[z.reshape(bsz, seq, d) for z in pre],
                           jnp.stack([rw_ln_g[j], rw_ln_b[j], rw_r_k[j].reshape(d)]), n_sub=n_sub)
            x2 = _rwkv_post(x2, y.reshape(bsz * seq, d), g[1:2], bf(rw_w_o[j]), tm=tm)
        x2 = _xattn(x2, g[2:4], bf(xa_wq[i]), kv_all[i], bf(xa_wo[i]), tm=tm, tiles_per_batch=tiles_per_batch)
        x2 = _ffn(x2, g[4:6], bf(ffn_w_in[i]), bf(ffn_w_out[i]), tm=tm)
    return x2.reshape(bsz, seq, d)
```

```python
import functools

import jax
import jax.numpy as jnp
from jax import lax
from jax.experimental import pallas as pl
from jax.experimental.pallas import tpu as pltpu

F32 = jnp.float32
BF16 = jnp.bfloat16

D_MODEL = 1024
XA_HEADS = 4
XA_HEAD_DIM = D_MODEL // XA_HEADS
GM_CHUNK = 128
GM_GROUPS = 8
GM_GROUP_DIM = D_MODEL // GM_GROUPS
RW_HEAD_DIM = 64
RW_HEADS = D_MODEL // RW_HEAD_DIM
RW_GN_EPS = 64e-5
NORM_EPS = 1e-12
RMS_EPS = 1e-6
LN_EPS = 1e-5

LANES = 128
SUBLANES = 8
RW_CHUNK = 64
RW_GROUPS = D_MODEL // LANES
VMEM_LIMIT = 56 * 1024 * 1024

HI = lax.Precision.HIGHEST


def _dot(a, b):
    return jnp.dot(a, b, preferred_element_type=F32)


def _dot_nt(a, b):
    return lax.dot_general(a, b, (((1,), (1,)), ((), ())), preferred_element_type=F32)


def _dot_tn(a, b):
    return lax.dot_general(a, b, (((0,), (0,)), ((), ())), preferred_element_type=F32)


def _rms(x, g):
    return x * lax.rsqrt(jnp.mean(x * x, axis=-1, keepdims=True) + RMS_EPS) * g


def _sigmoid(x):
    return 1.0 / (1.0 + jnp.exp(-x))


def _const_spec(shape):
    nd = len(shape)
    return pl.BlockSpec(shape, lambda *_: (0,) * nd, pipeline_mode=pl.Buffered(1))


def _params(n_axes=1):
    return pltpu.CompilerParams(dimension_semantics=("arbitrary",) * n_axes,
                                vmem_limit_bytes=VMEM_LIMIT)


def _memkv_kernel(mem_ref, g_ref, w_ref, o_ref):
    m = _rms(mem_ref[0], g_ref[0]).astype(BF16)
    o_ref[0, 0] = _dot(m, w_ref[0]).astype(BF16)


def _memkv(mem, gains, wkv):
    depth = wkv.shape[0]
    b, m, d = mem.shape
    return pl.pallas_call(
        _memkv_kernel,
        grid=(depth, b),
        in_specs=[
            pl.BlockSpec((1, m, d), lambda i, j: (j, 0, 0)),
            pl.BlockSpec((1, 1, d), lambda i, j: (i, 0, 0)),
            pl.BlockSpec((1, d, 2 * d), lambda i, j: (i, 0, 0)),
        ],
        out_specs=pl.BlockSpec((1, 1, m, 2 * d), lambda i, j: (i, j, 0, 0)),
        out_shape=jax.ShapeDtypeStruct((depth, b, m, 2 * d), BF16),
        compiler_params=_params(2),
        name="memkv",
    )(mem, gains.reshape(depth, 1, d), wkv)


def _xattn_kernel(x_ref, g_ref, wq_ref, kv_ref, wo_ref, o_ref):
    x = x_ref[...]
    h = _rms(x, g_ref[0:1]).astype(BF16)
    q = _dot(h, wq_ref[...]) * (XA_HEAD_DIM ** -0.5)
    heads = []
    for hd in range(XA_HEADS):
        lo = hd * XA_HEAD_DIM
        qh = q[:, lo:lo + XA_HEAD_DIM].astype(BF16)
        kh = kv_ref[0, :, lo:lo + XA_HEAD_DIM]
        vh = kv_ref[0, :, D_MODEL + lo:D_MODEL + lo + XA_HEAD_DIM]
        s = _dot_nt(qh, kh)
        p = jnp.exp(s - jnp.max(s, axis=-1, keepdims=True))
        p = p / jnp.sum(p, axis=-1, keepdims=True)
        heads.append(_dot(p.astype(BF16), vh).astype(BF16))
    o = jnp.concatenate(heads, axis=1)
    o_ref[...] = x + _rms(_dot(o, wo_ref[...]), g_ref[1:2])


def _xattn(x2, gains, wq, kv, wo, *, tm, tiles_per_batch):
    n, d = x2.shape
    m = kv.shape[1]
    return pl.pallas_call(
        _xattn_kernel,
        grid=(n // tm,),
        in_specs=[
            pl.BlockSpec((tm, d), lambda i: (i, 0)),
            _const_spec((2, d)),
            _const_spec((d, d)),
            pl.BlockSpec((1, m, 2 * d), lambda i: (i // tiles_per_batch, 0, 0)),
            _const_spec((d, d)),
        ],
        out_specs=pl.BlockSpec((tm, d), lambda i: (i, 0)),
        out_shape=jax.ShapeDtypeStruct((n, d), F32),
        compiler_params=_params(),
        name="xattn",
    )(x2, gains, wq, kv, wo)


def _ffn_kernel(x_ref, g_ref, win_ref, wout_ref, o_ref, *, d_ff, n_split):
    x = x_ref[...]
    h = _rms(x, g_ref[0:1]).astype(BF16)
    ck = d_ff // n_split
    acc = None
    for c in range(n_split):
        gate = _dot(h, win_ref[:, c * ck:(c + 1) * ck])
        up = _dot(h, win_ref[:, d_ff + c * ck:d_ff + (c + 1) * ck])
        act = (gate * _sigmoid(gate) * up).astype(BF16)
        part = _dot(act, wout_ref[c * ck:(c + 1) * ck, :])
        acc = part if acc is None else acc + part
    o_ref[...] = x + _rms(acc, g_ref[1:2])


def _ffn(x2, gains, w_in, w_out, *, tm):
    n, d = x2.shape
    d_ff = w_out.shape[0]
    n_split = 2 if d_ff % (2 * LANES) == 0 else 1
    return pl.pallas_call(
        functools.partial(_ffn_kernel, d_ff=d_ff, n_split=n_split),
        grid=(n // tm,),
        in_specs=[
            pl.BlockSpec((tm, d), lambda i: (i, 0)),
            _const_spec((2, d)),
            _const_spec((d, 2 * d_ff)),
            _const_spec((d_ff, d)),
        ],
        out_specs=pl.BlockSpec((tm, d), lambda i: (i, 0)),
        out_shape=jax.ShapeDtypeStruct((n, d), F32),
        compiler_params=_params(),
        name="ffn",
    )(x2, gains, w_in, w_out)


def _gmlp_kernel(x_ref, g_ref, win_ref, bin_ref, ln_ref, ws_ref, bs_ref, wout_ref, o_ref, mix_ref, *, tm):
    x = x_ref[...]
    h = _rms(x, g_ref[0:1]).astype(BF16)
    hh = _dot(h, win_ref[...]) + bin_ref[...]
    hh = 0.5 * hh * (1.0 + lax.erf(hh * (2.0 ** -0.5)))
    u = hh[:, :D_MODEL]
    v = hh[:, D_MODEL:]
    mu = jnp.mean(v, axis=-1, keepdims=True)
    vc = v - mu
    var = jnp.mean(vc * vc, axis=-1, keepdims=True)
    vn = (vc * lax.rsqrt(var + LN_EPS) * ln_ref[0:1] + ln_ref[1:2]).astype(BF16)
    for c in range(tm // GM_CHUNK):
        r0 = c * GM_CHUNK
        for g in range(GM_GROUPS):
            c0 = g * GM_GROUP_DIM
            mix_ref[r0:r0 + GM_CHUNK, c0:c0 + GM_GROUP_DIM] = (
                _dot(ws_ref[g], vn[r0:r0 + GM_CHUNK, c0:c0 + GM_GROUP_DIM]) + bs_ref[:, c0:c0 + GM_GROUP_DIM])
    gated = (u * mix_ref[...]).astype(BF16)
    o_ref[...] = x + _rms(_dot(gated, wout_ref[...]), g_ref[1:2])


def _gmlp(x2, gains, w_in, b_in, ln, w_s, b_s, w_out, *, tm):
    n, d = x2.shape
    return pl.pallas_call(
        functools.partial(_gmlp_kernel, tm=tm),
        grid=(n // tm,),
        in_specs=[
            pl.BlockSpec((tm, d), lambda i: (i, 0)),
            _const_spec((2, d)),
            _const_spec((d, 2 * d)),
            _const_spec((1, 2 * d)),
            _const_spec((2, d)),
            _const_spec((GM_GROUPS, GM_CHUNK, GM_CHUNK)),
            _const_spec((GM_CHUNK, d)),
            _const_spec((d, d)),
        ],
        out_specs=pl.BlockSpec((tm, d), lambda i: (i, 0)),
        out_shape=jax.ShapeDtypeStruct((n, d), F32),
        scratch_shapes=[pltpu.VMEM((tm, d), F32)],
        compiler_params=_params(),
        name="gmlp",
    )(x2, gains, w_in, b_in, ln, w_s, b_s, w_out)


def _rwkv_pre_kernel(*refs, tm, tiles_per_batch, has_vres):
    if has_vres:
        (x_ref, xp_ref, g_ref, mix_ref, wrkv_ref, vec_ref, w1_ref, w2_ref, a1_ref, a2_ref, g1_ref, g2_ref,
         vf_ref, v1_ref, v2_ref,
         r_ref, lw_ref, k_ref, v_ref, kk_ref, a_ref, gate_ref) = refs
    else:
        (x_ref, xp_ref, g_ref, mix_ref, wrkv_ref, vec_ref, w1_ref, w2_ref, a1_ref, a2_ref, g1_ref, g2_ref,
         r_ref, lw_ref, k_ref, v_ref, kk_ref, a_ref, gate_ref) = refs
    i = pl.program_id(0)
    gain = g_ref[0:1]
    h = _rms(x_ref[...], gain)
    hp = _rms(xp_ref[SUBLANES - 1:SUBLANES, :], gain)
    hp = jnp.where(i % tiles_per_batch == 0, 0.0, hp)
    row = lax.broadcasted_iota(jnp.int32, (tm, 1), 0)
    shifted = jnp.where(row == 0, hp, pltpu.roll(h, 1, axis=0))
    xx = shifted - h

    def mixed(p):
        return (h + xx * mix_ref[p:p + 1]).astype(BF16)

    w0, a0, k_k, k_a = vec_ref[0:1], vec_ref[1:2], vec_ref[2:3], vec_ref[3:4]
    r = _dot(mixed(0), wrkv_ref[0])
    k = _dot(mixed(1), wrkv_ref[1])
    xv = mixed(2)
    v = _dot(xv, wrkv_ref[2])
    z = w0 + _dot(jnp.tanh(_dot(mixed(3), w1_ref[...])).astype(BF16), w2_ref[...])
    w = -(jnp.maximum(-z, 0.0) + jnp.log(1.0 + jnp.exp(-jnp.abs(z)))) - 0.5
    a = _sigmoid(a0 + _dot(_dot(mixed(4), a1_ref[...]).astype(BF16), a2_ref[...]))
    gate = _dot(_sigmoid(_dot(mixed(5), g1_ref[...])).astype(BF16), g2_ref[...])
    if has_vres:
        v0 = vec_ref[4:5]
        sv = _sigmoid(v0 + _dot(_dot(xv, v1_ref[...]).astype(BF16), v2_ref[...]))
        v = v + (vf_ref[...] - v) * sv
    r_ref[...] = r.astype(BF16)
    lw_ref[...] = -jnp.exp(w)
    k_ref[...] = (k * (1.0 + (a - 1.0) * k_a)).astype(BF16)
    v_ref[...] = v
    kk_ref[...] = (k * k_k).astype(BF16)
    a_ref[...] = a.astype(BF16)
    gate_ref[...] = gate.astype(BF16)


def _rwkv_pre(x2, gain, mix, wrkv, vecs, w1, w2, a1, a2, g1, g2, vres, *, tm, tiles_per_batch):
    n, d = x2.shape
    has_vres = vres is not None
    tok = pl.BlockSpec((tm, d), lambda i: (i, 0))
    prev = pl.BlockSpec((SUBLANES, d), lambda i: (jnp.maximum(i * (tm // SUBLANES) - 1, 0), 0))
    in_specs = [tok, prev, _const_spec(gain.shape), _const_spec(mix.shape), _const_spec(wrkv.shape),
                _const_spec(vecs.shape), _const_spec(w1.shape), _const_spec(w2.shape), _const_spec(a1.shape),
                _const_spec(a2.shape), _const_spec(g1.shape), _const_spec(g2.shape)]
    args = [x2, x2, gain, mix, wrkv, vecs, w1, w2, a1, a2, g1, g2]
    if has_vres:
        v_first, v1, v2 = vres
        in_specs += [tok, _const_spec(v1.shape), _const_spec(v2.shape)]
        args += [v_first, v1, v2]
    return pl.pallas_call(
        functools.partial(_rwkv_pre_kernel, tm=tm, tiles_per_batch=tiles_per_batch, has_vres=has_vres),
        grid=(n // tm,),
        in_specs=in_specs,
        out_specs=[tok] * 7,
        out_shape=[jax.ShapeDtypeStruct((n, d), dt) for dt in (BF16, F32, BF16, F32, BF16, BF16, BF16)],
        compiler_params=_params(),
        name="rwkv_pre",
    )(*args)


def _rwkv_scan_kernel(r_ref, lw_ref, k_ref, v_ref, kk_ref, a_ref, gate_ref, vec_ref, y_ref, s_ref, *, n_sub, bsz):
    L = RW_CHUNK

    @pl.when(pl.program_id(0) == 0)
    def _():
        s_ref[...] = jnp.zeros_like(s_ref)

    ti = lax.broadcasted_iota(jnp.int32, (L, LANES), 0)
    si = lax.broadcasted_iota(jnp.int32, (L, LANES), 1) % L
    mask_strict = ti > si
    mask_incl2 = jnp.concatenate([ti >= si, ti >= si], axis=1)
    eye = (ti == si).astype(F32)
    level_masks = []
    s = 1
    while s < L:
        level_masks.append(((ti // (2 * s)) == (si // (2 * s))) & ((ti // s) > (si // s)))
        s *= 2
    tri = (lax.broadcasted_iota(jnp.int32, (L, L), 0) >= lax.broadcasted_iota(jnp.int32, (L, L), 1)).astype(F32)
    head0 = lax.broadcasted_iota(jnp.int32, (1, LANES), 1) < RW_HEAD_DIM
    same = ((lax.broadcasted_iota(jnp.int32, (LANES, LANES), 0) // RW_HEAD_DIM)
            == (lax.broadcasted_iota(jnp.int32, (LANES, LANES), 1) // RW_HEAD_DIM))

    def stack(z):
        return jnp.concatenate([jnp.where(head0, z, 0.0), jnp.where(head0, 0.0, z)], axis=0).astype(BF16)

    def head_sum(z):
        s0 = jnp.sum(jnp.where(head0, z, 0.0), axis=-1, keepdims=True)
        s1 = jnp.sum(jnp.where(head0, 0.0, z), axis=-1, keepdims=True)
        return jnp.where(head0, s0, s1)

    def chunk(j, carry):
        rows = pl.ds(pl.multiple_of(j * L, L), L)
        items = [(b, slice(g * LANES, (g + 1) * LANES)) for b in range(bsz) for g in range(RW_GROUPS)]
        lw_all = [lw_ref[b, rows, :] for b in range(bsz)]
        cum_all = [jnp.dot(tri, z, preferred_element_type=F32, precision=HI) for z in lw_all]
        cum = [cum_all[b][:, ln] for b, ln in items]
        w_in = [jnp.exp(c) for c in cum]
        w_ex = [jnp.exp(c - lw_all[b][:, ln]) for c, (b, ln) in zip(cum, items)]
        w_inv = [jnp.exp(-c) for c in cum]
        kk = [kk_ref[b, rows, ln].astype(F32) for b, ln in items]
        kk = [z / jnp.maximum(jnp.sqrt(head_sum(z * z)), NORM_EPS) for z in kk]
        r = [r_ref[b, rows, ln].astype(F32) for b, ln in items]
        k = [k_ref[b, rows, ln].astype(F32) for b, ln in items]
        x_ar = [jnp.concatenate([-z * we, rr * wi], axis=0).astype(BF16)
                for z, we, rr, wi in zip(kk, w_ex, r, w_in)]
        v = [v_ref[b, rows, ln] for b, ln in items]
        v_d = [stack(z) for z in v]
        bt = [z * a_ref[b, rows, ln].astype(F32) * w for z, w, (b, ln) in zip(kk, w_inv, items)]
        kt = [kx * w for kx, w in zip(k, w_inv)]
        aa = [_dot_nt(x, jnp.concatenate([stack(b_), stack(k_)], axis=0)) for x, b_, k_ in zip(x_ar, bt, kt)]
        a_ab = [jnp.where(mask_strict, z[:L, :LANES], 0.0) for z in aa]
        a_ak = [jnp.where(mask_strict, z[:L, LANES:], 0.0).astype(BF16) for z in aa]
        a_r = [jnp.where(mask_incl2, z[L:, :], 0.0).astype(BF16) for z in aa]
        tm = [eye + jnp.where(level_masks[0], a, 0.0) for a in a_ab]
        for lm in level_masks[1:]:
            ta = [_dot(t.astype(BF16), stack(jnp.where(lm, a, 0.0))).astype(BF16) for t, a in zip(tm, a_ab)]
            tm = [t + _dot(m, stack(t)) for t, m in zip(tm, ta)]
        s_bd = [s_ref[i] for i in range(len(items))]
        xs = [_dot_nt(x, s.astype(BF16)) for x, s in zip(x_ar, s_bd)]
        z = [q[:L] + _dot(a, vd) for q, a, vd in zip(xs, a_ak, v_d)]
        u = [_dot(t.astype(BF16), stack(zz)) for t, zz in zip(tm, z)]
        y = [q[L:] + _dot(ar, jnp.concatenate([stack(uu), vd], axis=0)) for q, ar, uu, vd in zip(xs, a_r, u, v_d)]
        ds = [_dot_tn(jnp.concatenate([uu, vv], axis=0).astype(BF16), jnp.concatenate([b_, k_], axis=0).astype(BF16))
              for uu, vv, b_, k_ in zip(u, v, bt, kt)]
        inv_n = 1.0 / RW_HEAD_DIM
        for i, (b, ln) in enumerate(items):
            s_ref[i] = (s_bd[i] + jnp.where(same, ds[i], 0.0)) * w_in[i][L - 1:L, :]
            yc = y[i] - head_sum(y[i]) * inv_n
            var = head_sum(yc * yc) * inv_n
            yn = yc * lax.rsqrt(var + RW_GN_EPS) * vec_ref[0:1, ln] + vec_ref[1:2, ln]
            bonus = head_sum(r[i] * k[i] * vec_ref[2:3, ln])
            out = (yn + bonus * v[i]) * gate_ref[b, rows, ln].astype(F32)
            y_ref[b, rows, ln] = out.astype(BF16)
        return carry

    lax.fori_loop(0, n_sub, chunk, 0, unroll=2)


def _rwkv_scan(r, lw, k, v, kk, a, gate, vecs, *, n_sub):
    bsz, t, d = r.shape
    lb = n_sub * RW_CHUNK
    spec = pl.BlockSpec((bsz, lb, d), lambda j: (0, j, 0))
    return pl.pallas_call(
        functools.partial(_rwkv_scan_kernel, n_sub=n_sub, bsz=bsz),
        grid=(t // lb,),
        in_specs=[spec] * 7 + [_const_spec(vecs.shape)],
        out_specs=spec,
        out_shape=jax.ShapeDtypeStruct((bsz, t, d), BF16),
        scratch_shapes=[pltpu.VMEM((bsz * RW_GROUPS, LANES, LANES), F32)],
        compiler_params=_params(),
        name="rwkv_scan",
    )(r, lw, k, v, kk, a, gate, vecs)


def _rwkv_post_kernel(x_ref, y_ref, g_ref, wo_ref, o_ref):
    o_ref[...] = x_ref[...] + _rms(_dot(y_ref[...], wo_ref[...]), g_ref[0:1])


def _rwkv_post(x2, y, gain, wo, *, tm):
    n, d = x2.shape
    tok = pl.BlockSpec((tm, d), lambda i: (i, 0))
    return pl.pallas_call(
        _rwkv_post_kernel,
        grid=(n // tm,),
        in_specs=[tok, tok, _const_spec(gain.shape), _const_spec(wo.shape)],
        out_specs=tok,
        out_shape=jax.ShapeDtypeStruct((n, d), F32),
        compiler_params=_params(),
        name="rwkv_post",
    )(x2, y, gain, wo)


def kernel(x, mem, norm_gains, mem_norm_gains, xa_wq, xa_wkv, xa_wo, ffn_w_in, ffn_w_out, gm_w_in, gm_b_in, gm_ln_g, gm_ln_b, gm_w_s, gm_b_s, gm_w_out, rw_mix, rw_w_rkv, rw_w0, rw_w1, rw_w2, rw_a0, rw_a1, rw_a2, rw_g1, rw_g2, rw_k_k, rw_k_a, rw_r_k, rw_ln_g, rw_ln_b, rw_w_o, rw_v0, rw_v1, rw_v2):
    bsz, seq, d = x.shape
    depth = norm_gains.shape[0]
    assert d == D_MODEL
    tm = min(512, seq)
    assert seq % tm == 0 and tm % GM_CHUNK == 0
    tiles_per_batch = seq // tm
    n_sub = min(4, seq // RW_CHUNK)
    assert seq % (n_sub * RW_CHUNK) == 0

    bf = lambda w: w.astype(BF16)
    kv_all = _memkv(mem, mem_norm_gains, bf(xa_wkv))

    x2 = x.reshape(bsz * seq, d)
    v_first = None
    for i in range(depth):
        g = norm_gains[i]
        j = i // 2
        if i % 2 == 0:
            x2 = _gmlp(x2, g[0:2], bf(gm_w_in[j]), gm_b_in[j][None, :],
                       jnp.stack([gm_ln_g[j], gm_ln_b[j]]), bf(jnp.tril(gm_w_s[j])),
                       jnp.repeat(gm_b_s[j].T, GM_GROUP_DIM, axis=1), bf(gm_w_out[j]), tm=tm)
        else:
            has_vres = j > 0
            vec_rows = [rw_w0[j], rw_a0[j], rw_k_k[j], rw_k_a[j]]
            vres = None
            if has_vres:
                vec_rows.append(rw_v0[j - 1])
                vres = (v_first, bf(rw_v1[j - 1]), bf(rw_v2[j - 1]))
            pre = _rwkv_pre(
                x2, g[0:1], rw_mix[j], bf(rw_w_rkv[j]), jnp.stack(vec_rows), bf(rw_w1[j]), bf(rw_w2[j]),
                bf(rw_a1[j]), bf(rw_a2[j]), bf(rw_g1[j]), bf(rw_g2[j]), vres,
                tm=tm, tiles_per_batch=tiles_per_batch)
            if not has_vres:
                v_first = pre[3]
            y = _rwkv_scan(*[z.reshape(bsz, seq, d) for z in pre],
                           jnp.stack([rw_ln_g[j], rw_ln_b[j], rw_r_k[j].reshape(d)]), n_sub=n_sub)
            x2 = _rwkv_post(x2, y.reshape(bsz * seq, d), g[1:2], bf(rw_w_o[j]), tm=tm)
        x2 = _xattn(x2, g[2:4], bf(xa_wq[i]), kv_all[i], bf(xa_wo[i]), tm=tm, tiles_per_batch=tiles_per_batch)
        x2 = _ffn(x2, g[4:6], bf(ffn_w_in[i]), bf(ffn_w_out[i]), tm=tm)
    return x2.reshape(bsz, seq, d)
```

```python
import functools

import jax
import jax.numpy as jnp
from jax import lax
from jax.experimental import pallas as pl
from jax.experimental.pallas import tpu as pltpu

F32 = jnp.float32
BF16 = jnp.bfloat16

D_MODEL = 1024
XA_HEADS = 4
XA_HEAD_DIM = D_MODEL // XA_HEADS
GM_CHUNK = 128
GM_GROUPS = 8
GM_GROUP_DIM = D_MODEL // GM_GROUPS
RW_HEAD_DIM = 64
RW_HEADS = D_MODEL // RW_HEAD_DIM
RW_GN_EPS = 64e-5
NORM_EPS = 1e-12
RMS_EPS = 1e-6
LN_EPS = 1e-5

LANES = 128
SUBLANES = 8
RW_CHUNK = 64
RW_GROUPS = D_MODEL // LANES
VMEM_LIMIT = 56 * 1024 * 1024

HI = lax.Precision.HIGHEST


def _dot(a, b):
    return jnp.dot(a, b, preferred_element_type=F32)


def _dot_nt(a, b):
    return lax.dot_general(a, b, (((1,), (1,)), ((), ())), preferred_element_type=F32)


def _dot_tn(a, b):
    return lax.dot_general(a, b, (((0,), (0,)), ((), ())), preferred_element_type=F32)


def _rms(x, g):
    return x * lax.rsqrt(jnp.mean(x * x, axis=-1, keepdims=True) + RMS_EPS) * g


def _sigmoid(x):
    return 1.0 / (1.0 + jnp.exp(-x))


def _const_spec(shape):
    nd = len(shape)
    return pl.BlockSpec(shape, lambda *_: (0,) * nd, pipeline_mode=pl.Buffered(1))


def _params(n_axes=1):
    return pltpu.CompilerParams(dimension_semantics=("arbitrary",) * n_axes,
                                vmem_limit_bytes=VMEM_LIMIT)


def _memkv_kernel(mem_ref, g_ref, w_ref, o_ref):
    m = _rms(mem_ref[0], g_ref[0]).astype(BF16)
    o_ref[0, 0] = _dot(m, w_ref[0]).astype(BF16)


def _memkv(mem, gains, wkv):
    depth = wkv.shape[0]
    b, m, d = mem.shape
    return pl.pallas_call(
        _memkv_kernel,
        grid=(depth, b),
        in_specs=[
            pl.BlockSpec((1, m, d), lambda i, j: (j, 0, 0)),
            pl.BlockSpec((1, 1, d), lambda i, j: (i, 0, 0)),
            pl.BlockSpec((1, d, 2 * d), lambda i, j: (i, 0, 0)),
        ],
        out_specs=pl.BlockSpec((1, 1, m, 2 * d), lambda i, j: (i, j, 0, 0)),
        out_shape=jax.ShapeDtypeStruct((depth, b, m, 2 * d), BF16),
        compiler_params=_params(2),
        name="memkv",
    )(mem, gains.reshape(depth, 1, d), wkv)


def _xattn_kernel(*refs, mixer_out):
    if mixer_out:
        x_ref, y_ref, gm_ref, wm_ref, g_ref, wq_ref, kv_ref, wo_ref, o_ref = refs
        x = x_ref[...] + _rms(_dot(y_ref[...], wm_ref[...]), gm_ref[0:1])
    else:
        x_ref, g_ref, wq_ref, kv_ref, wo_ref, o_ref = refs
        x = x_ref[...]
    h = _rms(x, g_ref[0:1]).astype(BF16)
    q = _dot(h, wq_ref[...]) * (XA_HEAD_DIM ** -0.5)
    heads = []
    for hd in range(XA_HEADS):
        lo = hd * XA_HEAD_DIM
        qh = q[:, lo:lo + XA_HEAD_DIM].astype(BF16)
        kh = kv_ref[0, :, lo:lo + XA_HEAD_DIM]
        vh = kv_ref[0, :, D_MODEL + lo:D_MODEL + lo + XA_HEAD_DIM]
        s = _dot_nt(qh, kh)
        p = jnp.exp(s - jnp.max(s, axis=-1, keepdims=True))
        p = p / jnp.sum(p, axis=-1, keepdims=True)
        heads.append(_dot(p.astype(BF16), vh).astype(BF16))
    o = jnp.concatenate(heads, axis=1)
    o_ref[...] = x + _rms(_dot(o, wo_ref[...]), g_ref[1:2])


def _xattn(x2, gains, wq, kv, wo, *, tm, tiles_per_batch, mixer=None):
    n, d = x2.shape
    m = kv.shape[1]
    tok = pl.BlockSpec((tm, d), lambda i: (i, 0))
    in_specs = [tok]
    args = [x2]
    if mixer is not None:
        in_specs += [tok, _const_spec((1, d)), _const_spec((d, d))]
        args += list(mixer)
    in_specs += [
        _const_spec((2, d)),
        _const_spec((d, d)),
        pl.BlockSpec((1, m, 2 * d), lambda i: (i // tiles_per_batch, 0, 0)),
        _const_spec((d, d)),
    ]
    args += [gains, wq, kv, wo]
    return pl.pallas_call(
        functools.partial(_xattn_kernel, mixer_out=mixer is not None),
        grid=(n // tm,),
        in_specs=in_specs,
        out_specs=tok,
        out_shape=jax.ShapeDtypeStruct((n, d), F32),
        compiler_params=_params(),
        name="xattn",
    )(*args)


def _ffn_kernel(x_ref, g_ref, win_ref, wout_ref, o_ref, *, d_ff, n_split):
    x = x_ref[...]
    h = _rms(x, g_ref[0:1]).astype(BF16)
    ck = d_ff // n_split
    acc = None
    for c in range(n_split):
        gate = _dot(h, win_ref[:, c * ck:(c + 1) * ck])
        up = _dot(h, win_ref[:, d_ff + c * ck:d_ff + (c + 1) * ck])
        act = (gate * _sigmoid(gate) * up).astype(BF16)
        part = _dot(act, wout_ref[c * ck:(c + 1) * ck, :])
        acc = part if acc is None else acc + part
    o_ref[...] = x + _rms(acc, g_ref[1:2])


def _ffn(x2, gains, w_in, w_out, *, tm):
    n, d = x2.shape
    d_ff = w_out.shape[0]
    n_split = 1
    return pl.pallas_call(
        functools.partial(_ffn_kernel, d_ff=d_ff, n_split=n_split),
        grid=(n // tm,),
        in_specs=[
            pl.BlockSpec((tm, d), lambda i: (i, 0)),
            _const_spec((2, d)),
            _const_spec((d, 2 * d_ff)),
            _const_spec((d_ff, d)),
        ],
        out_specs=pl.BlockSpec((tm, d), lambda i: (i, 0)),
        out_shape=jax.ShapeDtypeStruct((n, d), F32),
        compiler_params=_params(),
        name="ffn",
    )(x2, gains, w_in, w_out)


def _gmlp_kernel(x_ref, g_ref, win_ref, bin_ref, ln_ref, ws_ref, bs_ref, wout_ref, o_ref, mix_ref, *, tm):
    x = x_ref[...]
    h = _rms(x, g_ref[0:1]).astype(BF16)
    hh = _dot(h, win_ref[...]) + bin_ref[...]
    hh = 0.5 * hh * (1.0 + lax.erf(hh * (2.0 ** -0.5)))
    u = hh[:, :D_MODEL]
    v = hh[:, D_MODEL:]
    mu = jnp.mean(v, axis=-1, keepdims=True)
    vc = v - mu
    var = jnp.mean(vc * vc, axis=-1, keepdims=True)
    vn = (vc * lax.rsqrt(var + LN_EPS) * ln_ref[0:1] + ln_ref[1:2]).astype(BF16)
    for c in range(tm // GM_CHUNK):
        r0 = c * GM_CHUNK
        for g in range(GM_GROUPS):
            c0 = g * GM_GROUP_DIM
            mix_ref[r0:r0 + GM_CHUNK, c0:c0 + GM_GROUP_DIM] = (
                _dot(ws_ref[g], vn[r0:r0 + GM_CHUNK, c0:c0 + GM_GROUP_DIM]) + bs_ref[:, c0:c0 + GM_GROUP_DIM])
    gated = (u * mix_ref[...]).astype(BF16)
    o_ref[...] = x + _rms(_dot(gated, wout_ref[...]), g_ref[1:2])


def _gmlp(x2, gains, w_in, b_in, ln, w_s, b_s, w_out, *, tm):
    n, d = x2.shape
    return pl.pallas_call(
        functools.partial(_gmlp_kernel, tm=tm),
        grid=(n // tm,),
        in_specs=[
            pl.BlockSpec((tm, d), lambda i: (i, 0)),
            _const_spec((2, d)),
            _const_spec((d, 2 * d)),
            _const_spec((1, 2 * d)),
            _const_spec((2, d)),
            _const_spec((GM_GROUPS, GM_CHUNK, GM_CHUNK)),
            _const_spec((GM_CHUNK, d)),
            _const_spec((d, d)),
        ],
        out_specs=pl.BlockSpec((tm, d), lambda i: (i, 0)),
        out_shape=jax.ShapeDtypeStruct((n, d), F32),
        scratch_shapes=[pltpu.VMEM((tm, d), F32)],
        compiler_params=_params(),
        name="gmlp",
    )(x2, gains, w_in, b_in, ln, w_s, b_s, w_out)


def _rwkv_pre_kernel(*refs, tm, tiles_per_batch, has_vres):
    if has_vres:
        (x_ref, xp_ref, g_ref, mix_ref, wrkv_ref, vec_ref, w1_ref, w2_ref, a1_ref, a2_ref, g1_ref, g2_ref,
         vf_ref, v1_ref, v2_ref,
         r_ref, lw_ref, k_ref, v_ref, kk_ref, a_ref, gate_ref) = refs
    else:
        (x_ref, xp_ref, g_ref, mix_ref, wrkv_ref, vec_ref, w1_ref, w2_ref, a1_ref, a2_ref, g1_ref, g2_ref,
         r_ref, lw_ref, k_ref, v_ref, kk_ref, a_ref, gate_ref) = refs
    i = pl.program_id(0)
    gain = g_ref[0:1]
    h = _rms(x_ref[...], gain)
    hp = _rms(xp_ref[SUBLANES - 1:SUBLANES, :], gain)
    hp = jnp.where(i % tiles_per_batch == 0, 0.0, hp)
    row = lax.broadcasted_iota(jnp.int32, (tm, 1), 0)
    shifted = jnp.where(row == 0, hp, pltpu.roll(h, 1, axis=0))
    xx = shifted - h

    def mixed(p):
        return (h + xx * mix_ref[p:p + 1]).astype(BF16)

    w0, a0, k_k, k_a = vec_ref[0:1], vec_ref[1:2], vec_ref[2:3], vec_ref[3:4]
    r = _dot(mixed(0), wrkv_ref[0])
    k = _dot(mixed(1), wrkv_ref[1])
    xv = mixed(2)
    v = _dot(xv, wrkv_ref[2])
    z = w0 + _dot(jnp.tanh(_dot(mixed(3), w1_ref[...])).astype(BF16), w2_ref[...])
    w = -(jnp.maximum(-z, 0.0) + jnp.log(1.0 + jnp.exp(-jnp.abs(z)))) - 0.5
    a = _sigmoid(a0 + _dot(_dot(mixed(4), a1_ref[...]).astype(BF16), a2_ref[...]))
    gate = _dot(_sigmoid(_dot(mixed(5), g1_ref[...])).astype(BF16), g2_ref[...])
    if has_vres:
        v0 = vec_ref[4:5]
        sv = _sigmoid(v0 + _dot(_dot(xv, v1_ref[...]).astype(BF16), v2_ref[...]))
        v = v + (vf_ref[...] - v) * sv
    r_ref[...] = r.astype(BF16)
    lw_ref[...] = -jnp.exp(w)
    k_ref[...] = (k * (1.0 + (a - 1.0) * k_a)).astype(BF16)
    v_ref[...] = v
    kk_ref[...] = (k * k_k).astype(BF16)
    a_ref[...] = a.astype(BF16)
    gate_ref[...] = gate.astype(BF16)


def _rwkv_pre(x2, gain, mix, wrkv, vecs, w1, w2, a1, a2, g1, g2, vres, *, tm, tiles_per_batch):
    n, d = x2.shape
    has_vres = vres is not None
    tok = pl.BlockSpec((tm, d), lambda i: (i, 0))
    prev = pl.BlockSpec((SUBLANES, d), lambda i: (jnp.maximum(i * (tm // SUBLANES) - 1, 0), 0))
    in_specs = [tok, prev, _const_spec(gain.shape), _const_spec(mix.shape), _const_spec(wrkv.shape),
                _const_spec(vecs.shape), _const_spec(w1.shape), _const_spec(w2.shape), _const_spec(a1.shape),
                _const_spec(a2.shape), _const_spec(g1.shape), _const_spec(g2.shape)]
    args = [x2, x2, gain, mix, wrkv, vecs, w1, w2, a1, a2, g1, g2]
    if has_vres:
        v_first, v1, v2 = vres
        in_specs += [tok, _const_spec(v1.shape), _const_spec(v2.shape)]
        args += [v_first, v1, v2]
    return pl.pallas_call(
        functools.partial(_rwkv_pre_kernel, tm=tm, tiles_per_batch=tiles_per_batch, has_vres=has_vres),
        grid=(n // tm,),
        in_specs=in_specs,
        out_specs=[tok] * 7,
        out_shape=[jax.ShapeDtypeStruct((n, d), dt) for dt in (BF16, F32, BF16, F32, BF16, BF16, BF16)],
        compiler_params=_params(),
        name="rwkv_pre",
    )(*args)


def _rwkv_scan_kernel(r_ref, lw_ref, k_ref, v_ref, kk_ref, a_ref, gate_ref, vec_ref, y_ref, s_ref, *, n_sub, bsz):
    L = RW_CHUNK

    @pl.when(pl.program_id(0) == 0)
    def _():
        s_ref[...] = jnp.zeros_like(s_ref)

    ti = lax.broadcasted_iota(jnp.int32, (L, LANES), 0)
    si = lax.broadcasted_iota(jnp.int32, (L, LANES), 1) % L
    mask_strict = ti > si
    mask_incl2 = jnp.concatenate([ti >= si, ti >= si], axis=1)
    eye = (ti == si).astype(F32)
    level_masks = []
    s = 1
    while s < L:
        level_masks.append(((ti // (2 * s)) == (si // (2 * s))) & ((ti // s) > (si // s)))
        s *= 2
    tri = (lax.broadcasted_iota(jnp.int32, (L, L), 0) >= lax.broadcasted_iota(jnp.int32, (L, L), 1)).astype(F32)
    head0 = lax.broadcasted_iota(jnp.int32, (1, LANES), 1) < RW_HEAD_DIM
    same = ((lax.broadcasted_iota(jnp.int32, (LANES, LANES), 0) // RW_HEAD_DIM)
            == (lax.broadcasted_iota(jnp.int32, (LANES, LANES), 1) // RW_HEAD_DIM))

    def stack(z):
        return jnp.concatenate([jnp.where(head0, z, 0.0), jnp.where(head0, 0.0, z)], axis=0).astype(BF16)

    def head_sum(z):
        s0 = jnp.sum(jnp.where(head0, z, 0.0), axis=-1, keepdims=True)
        s1 = jnp.sum(jnp.where(head0, 0.0, z), axis=-1, keepdims=True)
        return jnp.where(head0, s0, s1)

    def chunk(j, carry):
        rows = pl.ds(pl.multiple_of(j * L, L), L)
        items = [(b, slice(g * LANES, (g + 1) * LANES)) for b in range(bsz) for g in range(RW_GROUPS)]
        lw_all = [lw_ref[b, rows, :] for b in range(bsz)]
        cum_all = [jnp.dot(tri, z, preferred_element_type=F32, precision=HI) for z in lw_all]
        cum = [cum_all[b][:, ln] for b, ln in items]
        w_in = [jnp.exp(c) for c in cum]
        w_ex = [jnp.exp(c - lw_all[b][:, ln]) for c, (b, ln) in zip(cum, items)]
        w_inv = [jnp.exp(-c) for c in cum]
        kk = [kk_ref[b, rows, ln].astype(F32) for b, ln in items]
        kk = [z / jnp.maximum(jnp.sqrt(head_sum(z * z)), NORM_EPS) for z in kk]
        r = [r_ref[b, rows, ln].astype(F32) for b, ln in items]
        k = [k_ref[b, rows, ln].astype(F32) for b, ln in items]
        x_ar = [jnp.concatenate([-z * we, rr * wi], axis=0).astype(BF16)
                for z, we, rr, wi in zip(kk, w_ex, r, w_in)]
        v = [v_ref[b, rows, ln] for b, ln in items]
        v_d = [stack(z) for z in v]
        bt = [z * a_ref[b, rows, ln].astype(F32) * w for z, w, (b, ln) in zip(kk, w_inv, items)]
        kt = [kx * w for kx, w in zip(k, w_inv)]
        aa = [_dot_nt(x, jnp.concatenate([stack(b_), stack(k_)], axis=0)) for x, b_, k_ in zip(x_ar, bt, kt)]
        a_ab = [jnp.where(mask_strict, z[:L, :LANES], 0.0) for z in aa]
        a_ak = [jnp.where(mask_strict, z[:L, LANES:], 0.0).astype(BF16) for z in aa]
        a_r = [jnp.where(mask_incl2, z[L:, :], 0.0).astype(BF16) for z in aa]
        tm = [eye + jnp.where(level_masks[0], a, 0.0) for a in a_ab]
        for lm in level_masks[1:]:
            ta = [_dot(t.astype(BF16), stack(jnp.where(lm, a, 0.0))).astype(BF16) for t, a in zip(tm, a_ab)]
            tm = [t + _dot(m, stack(t)) for t, m in zip(tm, ta)]
        s_bd = [s_ref[i] for i in range(len(items))]
        xs = [_dot_nt(x, s.astype(BF16)) for x, s in zip(x_ar, s_bd)]
        z = [q[:L] + _dot(a, vd) for q, a, vd in zip(xs, a_ak, v_d)]
        u = [_dot(t.astype(BF16), stack(zz)) for t, zz in zip(tm, z)]
        y = [q[L:] + _dot(ar, jnp.concatenate([stack(uu), vd], axis=0)) for q, ar, uu, vd in zip(xs, a_r, u, v_d)]
        ds = [_dot_tn(jnp.concatenate([uu, vv], axis=0).astype(BF16), jnp.concatenate([b_, k_], axis=0).astype(BF16))
              for uu, vv, b_, k_ in zip(u, v, bt, kt)]
        inv_n = 1.0 / RW_HEAD_DIM
        for i, (b, ln) in enumerate(items):
            s_ref[i] = (s_bd[i] + jnp.where(same, ds[i], 0.0)) * w_in[i][L - 1:L, :]
            yc = y[i] - head_sum(y[i]) * inv_n
            var = head_sum(yc * yc) * inv_n
            yn = yc * lax.rsqrt(var + RW_GN_EPS) * vec_ref[0:1, ln] + vec_ref[1:2, ln]
            bonus = head_sum(r[i] * k[i] * vec_ref[2:3, ln])
            out = (yn + bonus * v[i]) * gate_ref[b, rows, ln].astype(F32)
            y_ref[b, rows, ln] = out.astype(BF16)
        return carry

    lax.fori_loop(0, n_sub, chunk, 0, unroll=2)


def _rwkv_scan(r, lw, k, v, kk, a, gate, vecs, *, n_sub):
    bsz, t, d = r.shape
    lb = n_sub * RW_CHUNK
    spec = pl.BlockSpec((bsz, lb, d), lambda j: (0, j, 0))
    return pl.pallas_call(
        functools.partial(_rwkv_scan_kernel, n_sub=n_sub, bsz=bsz),
        grid=(t // lb,),
        in_specs=[spec] * 7 + [_const_spec(vecs.shape)],
        out_specs=spec,
        out_shape=jax.ShapeDtypeStruct((bsz, t, d), BF16),
        scratch_shapes=[pltpu.VMEM((bsz * RW_GROUPS, LANES, LANES), F32)],
        compiler_params=_params(),
        name="rwkv_scan",
    )(r, lw, k, v, kk, a, gate, vecs)


def kernel(x, mem, norm_gains, mem_norm_gains, xa_wq, xa_wkv, xa_wo, ffn_w_in, ffn_w_out, gm_w_in, gm_b_in, gm_ln_g, gm_ln_b, gm_w_s, gm_b_s, gm_w_out, rw_mix, rw_w_rkv, rw_w0, rw_w1, rw_w2, rw_a0, rw_a1, rw_a2, rw_g1, rw_g2, rw_k_k, rw_k_a, rw_r_k, rw_ln_g, rw_ln_b, rw_w_o, rw_v0, rw_v1, rw_v2):
    bsz, seq, d = x.shape
    depth = norm_gains.shape[0]
    assert d == D_MODEL
    tm = min(512, seq)
    assert seq % tm == 0 and tm % GM_CHUNK == 0
    tiles_per_batch = seq // tm
    n_sub = min(4, seq // RW_CHUNK)
    assert seq % (n_sub * RW_CHUNK) == 0

    bf = lambda w: w.astype(BF16)
    kv_all = _memkv(mem, mem_norm_gains, bf(xa_wkv))

    x2 = x.reshape(bsz * seq, d)
    v_first = None
    for i in range(depth):
        g = norm_gains[i]
        j = i // 2
        mixer = None
        if i % 2 == 0:
            x2 = _gmlp(x2, g[0:2], bf(gm_w_in[j]), gm_b_in[j][None, :],
                       jnp.stack([gm_ln_g[j], gm_ln_b[j]]), bf(jnp.tril(gm_w_s[j])),
                       jnp.repeat(gm_b_s[j].T, GM_GROUP_DIM, axis=1), bf(gm_w_out[j]), tm=tm)
        else:
            has_vres = j > 0
            vec_rows = [rw_w0[j], rw_a0[j], rw_k_k[j], rw_k_a[j]]
            vres = None
            if has_vres:
                vec_rows.append(rw_v0[j - 1])
                vres = (v_first, bf(rw_v1[j - 1]), bf(rw_v2[j - 1]))
            pre = _rwkv_pre(
                x2, g[0:1], rw_mix[j], bf(rw_w_rkv[j]), jnp.stack(vec_rows), bf(rw_w1[j]), bf(rw_w2[j]),
                bf(rw_a1[j]), bf(rw_a2[j]), bf(rw_g1[j]), bf(rw_g2[j]), vres,
                tm=tm, tiles_per_batch=tiles_per_batch)
            if not has_vres:
                v_first = pre[3]
            y = _rwkv_scan(*[z.reshape(bsz, seq, d) for z in pre],
                           jnp.stack([rw_ln_g[j], rw_ln_b[j], rw_r_k[j].reshape(d)]), n_sub=n_sub)
            mixer = (y.reshape(bsz * seq, d), g[1:2], bf(rw_w_o[j]))
        x2 = _xattn(x2, g[2:4], bf(xa_wq[i]), kv_all[i], bf(xa_wo[i]), tm=tm, tiles_per_batch=tiles_per_batch,
                    mixer=mixer)
        x2 = _ffn(x2, g[4:6], bf(ffn_w_in[i]), bf(ffn_w_out[i]), tm=tm)
    return x2.reshape(bsz, seq, d)
```

```python
import functools

import jax
import jax.numpy as jnp
from jax import lax
from jax.experimental import pallas as pl
from jax.experimental.pallas import tpu as pltpu

F32 = jnp.float32
BF16 = jnp.bfloat16

D_MODEL = 1024
XA_HEADS = 4
XA_HEAD_DIM = D_MODEL // XA_HEADS
GM_CHUNK = 128
GM_GROUPS = 8
GM_GROUP_DIM = D_MODEL // GM_GROUPS
RW_HEAD_DIM = 64
RW_HEADS = D_MODEL // RW_HEAD_DIM
RW_GN_EPS = 64e-5
NORM_EPS = 1e-12
RMS_EPS = 1e-6
LN_EPS = 1e-5

LANES = 128
SUBLANES = 8
RW_CHUNK = 64
RW_GROUPS = D_MODEL // LANES
VMEM_LIMIT = 56 * 1024 * 1024


def _dot(a, b):
    return jnp.dot(a, b, preferred_element_type=F32)


def _dot_nt(a, b):
    return lax.dot_general(a, b, (((1,), (1,)), ((), ())), preferred_element_type=F32)


def _dot_tn(a, b):
    return lax.dot_general(a, b, (((0,), (0,)), ((), ())), preferred_element_type=F32)


def _rms(x, g):
    return x * lax.rsqrt(jnp.mean(x * x, axis=-1, keepdims=True) + RMS_EPS) * g


def _sigmoid(x):
    return 1.0 / (1.0 + jnp.exp(-x))


def _const_spec(shape):
    nd = len(shape)
    return pl.BlockSpec(shape, lambda *_: (0,) * nd, pipeline_mode=pl.Buffered(1))


def _params(n_axes=1):
    return pltpu.CompilerParams(dimension_semantics=("arbitrary",) * n_axes,
                                vmem_limit_bytes=VMEM_LIMIT)


def _memkv_kernel(mem_ref, g_ref, w_ref, o_ref):
    m = _rms(mem_ref[0], g_ref[0]).astype(BF16)
    o_ref[0, 0] = _dot(m, w_ref[0]).astype(BF16)


def _memkv(mem, gains, wkv):
    depth = wkv.shape[0]
    b, m, d = mem.shape
    return pl.pallas_call(
        _memkv_kernel,
        grid=(depth, b),
        in_specs=[
            pl.BlockSpec((1, m, d), lambda i, j: (j, 0, 0)),
            pl.BlockSpec((1, 1, d), lambda i, j: (i, 0, 0)),
            pl.BlockSpec((1, d, 2 * d), lambda i, j: (i, 0, 0)),
        ],
        out_specs=pl.BlockSpec((1, 1, m, 2 * d), lambda i, j: (i, j, 0, 0)),
        out_shape=jax.ShapeDtypeStruct((depth, b, m, 2 * d), BF16),
        compiler_params=_params(2),
        name="memkv",
    )(mem, gains.reshape(depth, 1, d), wkv)


def _xattn_kernel(*refs, mixer_out):
    if mixer_out:
        x_ref, y_ref, gm_ref, wm_ref, g_ref, wq_ref, kv_ref, wo_ref, o_ref = refs
        x = x_ref[...] + _rms(_dot(y_ref[...], wm_ref[...]), gm_ref[0:1])
    else:
        x_ref, g_ref, wq_ref, kv_ref, wo_ref, o_ref = refs
        x = x_ref[...]
    h = _rms(x, g_ref[0:1]).astype(BF16)
    q = _dot(h, wq_ref[...]) * (XA_HEAD_DIM ** -0.5)
    heads = []
    for hd in range(XA_HEADS):
        lo = hd * XA_HEAD_DIM
        qh = q[:, lo:lo + XA_HEAD_DIM].astype(BF16)
        kh = kv_ref[0, :, lo:lo + XA_HEAD_DIM]
        vh = kv_ref[0, :, D_MODEL + lo:D_MODEL + lo + XA_HEAD_DIM]
        s = _dot_nt(qh, kh)
        p = jnp.exp(s - jnp.max(s, axis=-1, keepdims=True))
        p = p / jnp.sum(p, axis=-1, keepdims=True)
        heads.append(_dot(p.astype(BF16), vh).astype(BF16))
    o = jnp.concatenate(heads, axis=1)
    o_ref[...] = x + _rms(_dot(o, wo_ref[...]), g_ref[1:2])


def _xattn(x2, gains, wq, kv, wo, *, tm, tiles_per_batch, mixer=None):
    n, d = x2.shape
    m = kv.shape[1]
    tok = pl.BlockSpec((tm, d), lambda i: (i, 0))
    in_specs = [tok]
    args = [x2]
    if mixer is not None:
        in_specs += [tok, _const_spec((1, d)), _const_spec((d, d))]
        args += list(mixer)
    in_specs += [
        _const_spec((2, d)),
        _const_spec((d, d)),
        pl.BlockSpec((1, m, 2 * d), lambda i: (i // tiles_per_batch, 0, 0)),
        _const_spec((d, d)),
    ]
    args += [gains, wq, kv, wo]
    return pl.pallas_call(
        functools.partial(_xattn_kernel, mixer_out=mixer is not None),
        grid=(n // tm,),
        in_specs=in_specs,
        out_specs=tok,
        out_shape=jax.ShapeDtypeStruct((n, d), F32),
        compiler_params=_params(),
        name="xattn",
    )(*args)


def _ffn_kernel(x_ref, g_ref, win_ref, wout_ref, o_ref, *, d_ff, n_split):
    x = x_ref[...]
    h = _rms(x, g_ref[0:1]).astype(BF16)
    ck = d_ff // n_split
    acc = None
    for c in range(n_split):
        gate = _dot(h, win_ref[:, c * ck:(c + 1) * ck])
        up = _dot(h, win_ref[:, d_ff + c * ck:d_ff + (c + 1) * ck])
        act = (gate * _sigmoid(gate) * up).astype(BF16)
        part = _dot(act, wout_ref[c * ck:(c + 1) * ck, :])
        acc = part if acc is None else acc + part
    o_ref[...] = x + _rms(acc, g_ref[1:2])


def _ffn(x2, gains, w_in, w_out, *, tm):
    n, d = x2.shape
    d_ff = w_out.shape[0]
    n_split = 1
    return pl.pallas_call(
        functools.partial(_ffn_kernel, d_ff=d_ff, n_split=n_split),
        grid=(n // tm,),
        in_specs=[
            pl.BlockSpec((tm, d), lambda i: (i, 0)),
            _const_spec((2, d)),
            _const_spec((d, 2 * d_ff)),
            _const_spec((d_ff, d)),
        ],
        out_specs=pl.BlockSpec((tm, d), lambda i: (i, 0)),
        out_shape=jax.ShapeDtypeStruct((n, d), F32),
        compiler_params=_params(),
        name="ffn",
    )(x2, gains, w_in, w_out)


def _gmlp_kernel(x_ref, g_ref, win_ref, bin_ref, ln_ref, ws_ref, bs_ref, wout_ref, o_ref, mix_ref, *, tm):
    x = x_ref[...]
    h = _rms(x, g_ref[0:1]).astype(BF16)
    hh = _dot(h, win_ref[...]) + bin_ref[...]
    hh = 0.5 * hh * (1.0 + lax.erf(hh * (2.0 ** -0.5)))
    u = hh[:, :D_MODEL]
    v = hh[:, D_MODEL:]
    mu = jnp.mean(v, axis=-1, keepdims=True)
    vc = v - mu
    var = jnp.mean(vc * vc, axis=-1, keepdims=True)
    vn = (vc * lax.rsqrt(var + LN_EPS) * ln_ref[0:1] + ln_ref[1:2]).astype(BF16)
    for c in range(tm // GM_CHUNK):
        r0 = c * GM_CHUNK
        for g in range(GM_GROUPS):
            c0 = g * GM_GROUP_DIM
            mix_ref[r0:r0 + GM_CHUNK, c0:c0 + GM_GROUP_DIM] = (
                _dot(ws_ref[g], vn[r0:r0 + GM_CHUNK, c0:c0 + GM_GROUP_DIM]) + bs_ref[:, c0:c0 + GM_GROUP_DIM])
    gated = (u * mix_ref[...]).astype(BF16)
    o_ref[...] = x + _rms(_dot(gated, wout_ref[...]), g_ref[1:2])


def _gmlp(x2, gains, w_in, b_in, ln, w_s, b_s, w_out, *, tm):
    n, d = x2.shape
    return pl.pallas_call(
        functools.partial(_gmlp_kernel, tm=tm),
        grid=(n // tm,),
        in_specs=[
            pl.BlockSpec((tm, d), lambda i: (i, 0)),
            _const_spec((2, d)),
            _const_spec((d, 2 * d)),
            _const_spec((1, 2 * d)),
            _const_spec((2, d)),
            _const_spec((GM_GROUPS, GM_CHUNK, GM_CHUNK)),
            _const_spec((GM_CHUNK, d)),
            _const_spec((d, d)),
        ],
        out_specs=pl.BlockSpec((tm, d), lambda i: (i, 0)),
        out_shape=jax.ShapeDtypeStruct((n, d), F32),
        scratch_shapes=[pltpu.VMEM((tm, d), F32)],
        compiler_params=_params(),
        name="gmlp",
    )(x2, gains, w_in, b_in, ln, w_s, b_s, w_out)


def _rwkv_pre_kernel(*refs, tm, tiles_per_batch, has_vres):
    if has_vres:
        (x_ref, xp_ref, g_ref, mix_ref, wrkv_ref, vec_ref, w1_ref, w2_ref, a1_ref, a2_ref, g1_ref, g2_ref,
         vf_ref, v1_ref, v2_ref,
         r_ref, lw_ref, k_ref, v_ref, kk_ref, a_ref, gate_ref) = refs
    else:
        (x_ref, xp_ref, g_ref, mix_ref, wrkv_ref, vec_ref, w1_ref, w2_ref, a1_ref, a2_ref, g1_ref, g2_ref,
         r_ref, lw_ref, k_ref, v_ref, kk_ref, a_ref, gate_ref) = refs
    i = pl.program_id(0)
    gain = g_ref[0:1]
    h = _rms(x_ref[...], gain)
    hp = _rms(xp_ref[SUBLANES - 1:SUBLANES, :], gain)
    hp = jnp.where(i % tiles_per_batch == 0, 0.0, hp)
    row = lax.broadcasted_iota(jnp.int32, (tm, 1), 0)
    shifted = jnp.where(row == 0, hp, pltpu.roll(h, 1, axis=0))
    xx = shifted - h

    def mixed(p):
        return (h + xx * mix_ref[p:p + 1]).astype(BF16)

    w0, a0, k_k, k_a = vec_ref[0:1], vec_ref[1:2], vec_ref[2:3], vec_ref[3:4]
    r = _dot(mixed(0), wrkv_ref[0])
    k = _dot(mixed(1), wrkv_ref[1])
    xv = mixed(2)
    v = _dot(xv, wrkv_ref[2])
    z = w0 + _dot(jnp.tanh(_dot(mixed(3), w1_ref[...])).astype(BF16), w2_ref[...])
    w = -(jnp.maximum(-z, 0.0) + jnp.log(1.0 + jnp.exp(-jnp.abs(z)))) - 0.5
    a = _sigmoid(a0 + _dot(_dot(mixed(4), a1_ref[...]).astype(BF16), a2_ref[...]))
    gate = _dot(_sigmoid(_dot(mixed(5), g1_ref[...])).astype(BF16), g2_ref[...])
    if has_vres:
        v0 = vec_ref[4:5]
        sv = _sigmoid(v0 + _dot(_dot(xv, v1_ref[...]).astype(BF16), v2_ref[...]))
        v = v + (vf_ref[...] - v) * sv
    r_ref[...] = r.astype(BF16)
    lw_ref[...] = -jnp.exp(w)
    k_ref[...] = (k * (1.0 + (a - 1.0) * k_a)).astype(BF16)
    v_ref[...] = v
    kk_ref[...] = (k * k_k).astype(BF16)
    a_ref[...] = a.astype(BF16)
    gate_ref[...] = gate.astype(BF16)


def _rwkv_pre(x2, gain, mix, wrkv, vecs, w1, w2, a1, a2, g1, g2, vres, *, tm, tiles_per_batch):
    n, d = x2.shape
    has_vres = vres is not None
    tok = pl.BlockSpec((tm, d), lambda i: (i, 0))
    prev = pl.BlockSpec((SUBLANES, d), lambda i: (jnp.maximum(i * (tm // SUBLANES) - 1, 0), 0))
    in_specs = [tok, prev, _const_spec(gain.shape), _const_spec(mix.shape), _const_spec(wrkv.shape),
                _const_spec(vecs.shape), _const_spec(w1.shape), _const_spec(w2.shape), _const_spec(a1.shape),
                _const_spec(a2.shape), _const_spec(g1.shape), _const_spec(g2.shape)]
    args = [x2, x2, gain, mix, wrkv, vecs, w1, w2, a1, a2, g1, g2]
    if has_vres:
        v_first, v1, v2 = vres
        in_specs += [tok, _const_spec(v1.shape), _const_spec(v2.shape)]
        args += [v_first, v1, v2]
    return pl.pallas_call(
        functools.partial(_rwkv_pre_kernel, tm=tm, tiles_per_batch=tiles_per_batch, has_vres=has_vres),
        grid=(n // tm,),
        in_specs=in_specs,
        out_specs=[tok] * 7,
        out_shape=[jax.ShapeDtypeStruct((n, d), dt) for dt in (BF16, F32, BF16, F32, BF16, BF16, BF16)],
        compiler_params=_params(),
        name="rwkv_pre",
    )(*args)


def _rwkv_scan_kernel(r_ref, lw_ref, k_ref, v_ref, kk_ref, a_ref, gate_ref, vec_ref, y_ref, s_ref, *, n_sub, bsz):
    L = RW_CHUNK

    @pl.when(pl.program_id(0) == 0)
    def _():
        s_ref[...] = jnp.zeros_like(s_ref)

    ti = lax.broadcasted_iota(jnp.int32, (L, LANES), 0)
    si = lax.broadcasted_iota(jnp.int32, (L, LANES), 1) % L
    mask_strict = ti > si
    mask_incl2 = jnp.concatenate([ti >= si, ti >= si], axis=1)
    eye = (ti == si).astype(F32)
    level_masks = []
    s = 1
    while s < L:
        level_masks.append(((ti // (2 * s)) == (si // (2 * s))) & ((ti // s) > (si // s)))
        s *= 2
    tri = (lax.broadcasted_iota(jnp.int32, (L, L), 0) >= lax.broadcasted_iota(jnp.int32, (L, L), 1)).astype(BF16)
    head0 = lax.broadcasted_iota(jnp.int32, (1, LANES), 1) < RW_HEAD_DIM
    same = ((lax.broadcasted_iota(jnp.int32, (LANES, LANES), 0) // RW_HEAD_DIM)
            == (lax.broadcasted_iota(jnp.int32, (LANES, LANES), 1) // RW_HEAD_DIM))

    def cumsum(z):
        hi = z.astype(BF16)
        r1 = z - hi.astype(F32)
        mid = r1.astype(BF16)
        lo = (r1 - mid.astype(F32)).astype(BF16)
        return _dot(tri, hi) + (_dot(tri, mid) + _dot(tri, lo))

    def stack(z):
        return jnp.concatenate([jnp.where(head0, z, 0.0), jnp.where(head0, 0.0, z)], axis=0).astype(BF16)

    def head_sum(z):
        s0 = jnp.sum(jnp.where(head0, z, 0.0), axis=-1, keepdims=True)
        s1 = jnp.sum(jnp.where(head0, 0.0, z), axis=-1, keepdims=True)
        return jnp.where(head0, s0, s1)

    def chunk(j, carry):
        rows = pl.ds(pl.multiple_of(j * L, L), L)
        items = [(b, slice(g * LANES, (g + 1) * LANES)) for b in range(bsz) for g in range(RW_GROUPS)]
        lw_all = [lw_ref[b, rows, :] for b in range(bsz)]
        cum_all = [cumsum(z) for z in lw_all]
        cum = [cum_all[b][:, ln] for b, ln in items]
        w_in = [jnp.exp(c) for c in cum]
        w_ex = [jnp.exp(c - lw_all[b][:, ln]) for c, (b, ln) in zip(cum, items)]
        w_inv = [jnp.exp(-c) for c in cum]
        kk = [kk_ref[b, rows, ln].astype(F32) for b, ln in items]
        kk = [z / jnp.maximum(jnp.sqrt(head_sum(z * z)), NORM_EPS) for z in kk]
        r = [r_ref[b, rows, ln].astype(F32) for b, ln in items]
        k = [k_ref[b, rows, ln].astype(F32) for b, ln in items]
        x_ar = [jnp.concatenate([-z * we, rr * wi], axis=0).astype(BF16)
                for z, we, rr, wi in zip(kk, w_ex, r, w_in)]
        v = [v_ref[b, rows, ln] for b, ln in items]
        v_d = [stack(z) for z in v]
        bt = [z * a_ref[b, rows, ln].astype(F32) * w for z, w, (b, ln) in zip(kk, w_inv, items)]
        kt = [kx * w for kx, w in zip(k, w_inv)]
        aa = [_dot_nt(x, jnp.concatenate([stack(b_), stack(k_)], axis=0)) for x, b_, k_ in zip(x_ar, bt, kt)]
        a_ab = [jnp.where(mask_strict, z[:L, :LANES], 0.0) for z in aa]
        a_ak = [jnp.where(mask_strict, z[:L, LANES:], 0.0).astype(BF16) for z in aa]
        a_r = [jnp.where(mask_incl2, z[L:, :], 0.0).astype(BF16) for z in aa]
        tm = [eye + jnp.where(level_masks[0], a, 0.0) for a in a_ab]
        for lm in level_masks[1:]:
            ta = [_dot(t.astype(BF16), stack(jnp.where(lm, a, 0.0))).astype(BF16) for t, a in zip(tm, a_ab)]
            tm = [t + _dot(m, stack(t)) for t, m in zip(tm, ta)]
        s_bd = [s_ref[i] for i in range(len(items))]
        xs = [_dot_nt(x, s.astype(BF16)) for x, s in zip(x_ar, s_bd)]
        z = [q[:L] + _dot(a, vd) for q, a, vd in zip(xs, a_ak, v_d)]
        u = [_dot(t.astype(BF16), stack(zz)) for t, zz in zip(tm, z)]
        y = [q[L:] + _dot(ar, jnp.concatenate([stack(uu), vd], axis=0)) for q, ar, uu, vd in zip(xs, a_r, u, v_d)]
        ds = [_dot_tn(jnp.concatenate([uu, vv], axis=0).astype(BF16), jnp.concatenate([b_, k_], axis=0).astype(BF16))
              for uu, vv, b_, k_ in zip(u, v, bt, kt)]
        inv_n = 1.0 / RW_HEAD_DIM
        for i, (b, ln) in enumerate(items):
            s_ref[i] = (s_bd[i] + jnp.where(same, ds[i], 0.0)) * w_in[i][L - 1:L, :]
            yc = y[i] - head_sum(y[i]) * inv_n
            var = head_sum(yc * yc) * inv_n
            yn = yc * lax.rsqrt(var + RW_GN_EPS) * vec_ref[0:1, ln] + vec_ref[1:2, ln]
            bonus = head_sum(r[i] * k[i] * vec_ref[2:3, ln])
            out = (yn + bonus * v[i]) * gate_ref[b, rows, ln].astype(F32)
            y_ref[b, rows, ln] = out.astype(BF16)
        return carry

    lax.fori_loop(0, n_sub, chunk, 0, unroll=True)


def _rwkv_scan(r, lw, k, v, kk, a, gate, vecs, *, n_sub):
    bsz, t, d = r.shape
    lb = n_sub * RW_CHUNK
    spec = pl.BlockSpec((bsz, lb, d), lambda j: (0, j, 0))
    return pl.pallas_call(
        functools.partial(_rwkv_scan_kernel, n_sub=n_sub, bsz=bsz),
        grid=(t // lb,),
        in_specs=[spec] * 7 + [_const_spec(vecs.shape)],
        out_specs=spec,
        out_shape=jax.ShapeDtypeStruct((bsz, t, d), BF16),
        scratch_shapes=[pltpu.VMEM((bsz * RW_GROUPS, LANES, LANES), F32)],
        compiler_params=_params(),
        name="rwkv_scan",
    )(r, lw, k, v, kk, a, gate, vecs)


def kernel(x, mem, norm_gains, mem_norm_gains, xa_wq, xa_wkv, xa_wo, ffn_w_in, ffn_w_out, gm_w_in, gm_b_in, gm_ln_g, gm_ln_b, gm_w_s, gm_b_s, gm_w_out, rw_mix, rw_w_rkv, rw_w0, rw_w1, rw_w2, rw_a0, rw_a1, rw_a2, rw_g1, rw_g2, rw_k_k, rw_k_a, rw_r_k, rw_ln_g, rw_ln_b, rw_w_o, rw_v0, rw_v1, rw_v2):
    bsz, seq, d = x.shape
    depth = norm_gains.shape[0]
    assert d == D_MODEL
    tm = min(512, seq)
    assert seq % tm == 0 and tm % GM_CHUNK == 0
    tiles_per_batch = seq // tm
    n_sub = min(4, seq // RW_CHUNK)
    assert seq % (n_sub * RW_CHUNK) == 0

    bf = lambda w: w.astype(BF16)
    kv_all = _memkv(mem, mem_norm_gains, bf(xa_wkv))

    x2 = x.reshape(bsz * seq, d)
    v_first = None
    for i in range(depth):
        g = norm_gains[i]
        j = i // 2
        mixer = None
        if i % 2 == 0:
            x2 = _gmlp(x2, g[0:2], bf(gm_w_in[j]), gm_b_in[j][None, :],
                       jnp.stack([gm_ln_g[j], gm_ln_b[j]]), bf(jnp.tril(gm_w_s[j])),
                       jnp.repeat(gm_b_s[j].T, GM_GROUP_DIM, axis=1), bf(gm_w_out[j]), tm=tm)
        else:
            has_vres = j > 0
            vec_rows = [rw_w0[j], rw_a0[j], rw_k_k[j], rw_k_a[j]]
            vres = None
            if has_vres:
                vec_rows.append(rw_v0[j - 1])
                vres = (v_first, bf(rw_v1[j - 1]), bf(rw_v2[j - 1]))
            pre = _rwkv_pre(
                x2, g[0:1], rw_mix[j], bf(rw_w_rkv[j]), jnp.stack(vec_rows), bf(rw_w1[j]), bf(rw_w2[j]),
                bf(rw_a1[j]), bf(rw_a2[j]), bf(rw_g1[j]), bf(rw_g2[j]), vres,
                tm=tm, tiles_per_batch=tiles_per_batch)
            if not has_vres:
                v_first = pre[3]
            y = _rwkv_scan(*[z.reshape(bsz, seq, d) for z in pre],
                           jnp.stack([rw_ln_g[j], rw_ln_b[j], rw_r_k[j].reshape(d)]), n_sub=n_sub)
            mixer = (y.reshape(bsz * seq, d), g[1:2], bf(rw_w_o[j]))
        x2 = _xattn(x2, g[2:4], bf(xa_wq[i]), kv_all[i], bf(xa_wo[i]), tm=tm, tiles_per_batch=tiles_per_batch,
                    mixer=mixer)
        x2 = _ffn(x2, g[4:6], bf(ffn_w_in[i]), bf(ffn_w_out[i]), tm=tm)
    return x2.reshape(bsz, seq, d)
```

```python
import functools

import jax
import jax.numpy as jnp
from jax import lax
from jax.experimental import pallas as pl
from jax.experimental.pallas import tpu as pltpu

F32 = jnp.float32
BF16 = jnp.bfloat16

D_MODEL = 1024
XA_HEADS = 4
XA_HEAD_DIM = D_MODEL // XA_HEADS
GM_CHUNK = 128
GM_GROUPS = 8
GM_GROUP_DIM = D_MODEL // GM_GROUPS
RW_HEAD_DIM = 64
RW_HEADS = D_MODEL // RW_HEAD_DIM
RW_GN_EPS = 64e-5
NORM_EPS = 1e-12
RMS_EPS = 1e-6
LN_EPS = 1e-5

LANES = 128
SUBLANES = 8
RW_CHUNK = 64
RW_GROUPS = D_MODEL // LANES
VMEM_LIMIT = 56 * 1024 * 1024


def _dot(a, b):
    return jnp.dot(a, b, preferred_element_type=F32)


def _dot_nt(a, b):
    return lax.dot_general(a, b, (((1,), (1,)), ((), ())), preferred_element_type=F32)


def _dot_tn(a, b):
    return lax.dot_general(a, b, (((0,), (0,)), ((), ())), preferred_element_type=F32)


def _rms(x, g):
    return x * lax.rsqrt(jnp.mean(x * x, axis=-1, keepdims=True) + RMS_EPS) * g


def _sigmoid(x):
    return 1.0 / (1.0 + jnp.exp(-x))


def _const_spec(shape):
    nd = len(shape)
    return pl.BlockSpec(shape, lambda *_: (0,) * nd, pipeline_mode=pl.Buffered(1))


def _params(n_axes=1):
    return pltpu.CompilerParams(dimension_semantics=("arbitrary",) * n_axes,
                                vmem_limit_bytes=VMEM_LIMIT)


def _memkv_kernel(mem_ref, g_ref, w_ref, o_ref):
    m = _rms(mem_ref[0], g_ref[0]).astype(BF16)
    o_ref[0, 0] = _dot(m, w_ref[0]).astype(BF16)


def _memkv(mem, gains, wkv):
    depth = wkv.shape[0]
    b, m, d = mem.shape
    return pl.pallas_call(
        _memkv_kernel,
        grid=(depth, b),
        in_specs=[
            pl.BlockSpec((1, m, d), lambda i, j: (j, 0, 0)),
            pl.BlockSpec((1, 1, d), lambda i, j: (i, 0, 0)),
            pl.BlockSpec((1, d, 2 * d), lambda i, j: (i, 0, 0)),
        ],
        out_specs=pl.BlockSpec((1, 1, m, 2 * d), lambda i, j: (i, j, 0, 0)),
        out_shape=jax.ShapeDtypeStruct((depth, b, m, 2 * d), BF16),
        compiler_params=_params(2),
        name="memkv",
    )(mem, gains.reshape(depth, 1, d), wkv)


def _xattn_kernel(*refs, mixer_out):
    if mixer_out:
        x_ref, y_ref, gm_ref, wm_ref, g_ref, wq_ref, kv_ref, wo_ref, o_ref = refs
        x = x_ref[...] + _rms(_dot(y_ref[...], wm_ref[...]), gm_ref[0:1])
    else:
        x_ref, g_ref, wq_ref, kv_ref, wo_ref, o_ref = refs
        x = x_ref[...]
    h = _rms(x, g_ref[0:1]).astype(BF16)
    q = _dot(h, wq_ref[...]) * (XA_HEAD_DIM ** -0.5)
    heads = []
    for hd in range(XA_HEADS):
        lo = hd * XA_HEAD_DIM
        qh = q[:, lo:lo + XA_HEAD_DIM].astype(BF16)
        kh = kv_ref[0, :, lo:lo + XA_HEAD_DIM]
        vh = kv_ref[0, :, D_MODEL + lo:D_MODEL + lo + XA_HEAD_DIM]
        s = _dot_nt(qh, kh)
        p = jnp.exp(s - jnp.max(s, axis=-1, keepdims=True))
        p = p / jnp.sum(p, axis=-1, keepdims=True)
        heads.append(_dot(p.astype(BF16), vh).astype(BF16))
    o = jnp.concatenate(heads, axis=1)
    o_ref[...] = x + _rms(_dot(o, wo_ref[...]), g_ref[1:2])


def _xattn(x2, gains, wq, kv, wo, *, tm, tiles_per_batch, mixer=None):
    n, d = x2.shape
    m = kv.shape[1]
    tok = pl.BlockSpec((tm, d), lambda i: (i, 0))
    in_specs = [tok]
    args = [x2]
    if mixer is not None:
        in_specs += [tok, _const_spec((1, d)), _const_spec((d, d))]
        args += list(mixer)
    in_specs += [
        _const_spec((2, d)),
        _const_spec((d, d)),
        pl.BlockSpec((1, m, 2 * d), lambda i: (i // tiles_per_batch, 0, 0)),
        _const_spec((d, d)),
    ]
    args += [gains, wq, kv, wo]
    return pl.pallas_call(
        functools.partial(_xattn_kernel, mixer_out=mixer is not None),
        grid=(n // tm,),
        in_specs=in_specs,
        out_specs=tok,
        out_shape=jax.ShapeDtypeStruct((n, d), F32),
        compiler_params=_params(),
        name="xattn",
    )(*args)


def _ffn_kernel(x_ref, g_ref, win_ref, wout_ref, o_ref, *, d_ff, n_split):
    assert n_split == 1
    half = x_ref.shape[0] // 2
    parts = [slice(0, half), slice(half, 2 * half)]
    x = [x_ref[rs, :] for rs in parts]
    h = [_rms(z, g_ref[0:1]).astype(BF16) for z in x]
    gate = [_dot(z, win_ref[:, :d_ff]) for z in h]
    up = [_dot(z, win_ref[:, d_ff:]) for z in h]
    act = [(g * _sigmoid(g) * u).astype(BF16) for g, u in zip(gate, up)]
    out = [_dot(z, wout_ref[...]) for z in act]
    for rs, xz, oz in zip(parts, x, out):
        o_ref[rs, :] = xz + _rms(oz, g_ref[1:2])


def _ffn(x2, gains, w_in, w_out, *, tm):
    n, d = x2.shape
    d_ff = w_out.shape[0]
    n_split = 1
    return pl.pallas_call(
        functools.partial(_ffn_kernel, d_ff=d_ff, n_split=n_split),
        grid=(n // tm,),
        in_specs=[
            pl.BlockSpec((tm, d), lambda i: (i, 0)),
            _const_spec((2, d)),
            _const_spec((d, 2 * d_ff)),
            _const_spec((d_ff, d)),
        ],
        out_specs=pl.BlockSpec((tm, d), lambda i: (i, 0)),
        out_shape=jax.ShapeDtypeStruct((n, d), F32),
        compiler_params=_params(),
        name="ffn",
    )(x2, gains, w_in, w_out)


def _gmlp_kernel(x_ref, g_ref, win_ref, bin_ref, ln_ref, ws_ref, bs_ref, wout_ref, o_ref, mix_ref, *, tm):
    x = x_ref[...]
    h = _rms(x, g_ref[0:1]).astype(BF16)
    hh = _dot(h, win_ref[...]) + bin_ref[...]
    hh = 0.5 * hh * (1.0 + lax.erf(hh * (2.0 ** -0.5)))
    u = hh[:, :D_MODEL]
    v = hh[:, D_MODEL:]
    mu = jnp.mean(v, axis=-1, keepdims=True)
    vc = v - mu
    var = jnp.mean(vc * vc, axis=-1, keepdims=True)
    vn = (vc * lax.rsqrt(var + LN_EPS) * ln_ref[0:1] + ln_ref[1:2]).astype(BF16)
    for c in range(tm // GM_CHUNK):
        r0 = c * GM_CHUNK
        for g in range(GM_GROUPS):
            c0 = g * GM_GROUP_DIM
            mix_ref[r0:r0 + GM_CHUNK, c0:c0 + GM_GROUP_DIM] = (
                _dot(ws_ref[g], vn[r0:r0 + GM_CHUNK, c0:c0 + GM_GROUP_DIM]) + bs_ref[:, c0:c0 + GM_GROUP_DIM])
    gated = (u * mix_ref[...]).astype(BF16)
    o_ref[...] = x + _rms(_dot(gated, wout_ref[...]), g_ref[1:2])


def _gmlp(x2, gains, w_in, b_in, ln, w_s, b_s, w_out, *, tm):
    n, d = x2.shape
    return pl.pallas_call(
        functools.partial(_gmlp_kernel, tm=tm),
        grid=(n // tm,),
        in_specs=[
            pl.BlockSpec((tm, d), lambda i: (i, 0)),
            _const_spec((2, d)),
            _const_spec((d, 2 * d)),
            _const_spec((1, 2 * d)),
            _const_spec((2, d)),
            _const_spec((GM_GROUPS, GM_CHUNK, GM_CHUNK)),
            _const_spec((GM_CHUNK, d)),
            _const_spec((d, d)),
        ],
        out_specs=pl.BlockSpec((tm, d), lambda i: (i, 0)),
        out_shape=jax.ShapeDtypeStruct((n, d), F32),
        scratch_shapes=[pltpu.VMEM((tm, d), F32)],
        compiler_params=_params(),
        name="gmlp",
    )(x2, gains, w_in, b_in, ln, w_s, b_s, w_out)


def _rwkv_pre_kernel(*refs, tm, tiles_per_batch, has_vres):
    if has_vres:
        (x_ref, xp_ref, g_ref, mix_ref, wrkv_ref, vec_ref, w1_ref, w2_ref, a1_ref, a2_ref, g1_ref, g2_ref,
         vf_ref, v1_ref, v2_ref,
         r_ref, lw_ref, k_ref, v_ref, kk_ref, a_ref, gate_ref) = refs
    else:
        (x_ref, xp_ref, g_ref, mix_ref, wrkv_ref, vec_ref, w1_ref, w2_ref, a1_ref, a2_ref, g1_ref, g2_ref,
         r_ref, lw_ref, k_ref, v_ref, kk_ref, a_ref, gate_ref) = refs
    i = pl.program_id(0)
    gain = g_ref[0:1]
    h = _rms(x_ref[...], gain)
    hp = _rms(xp_ref[SUBLANES - 1:SUBLANES, :], gain)
    hp = jnp.where(i % tiles_per_batch == 0, 0.0, hp)
    row = lax.broadcasted_iota(jnp.int32, (tm, 1), 0)
    shifted = jnp.where(row == 0, hp, pltpu.roll(h, 1, axis=0))
    xx = shifted - h

    def mixed(p):
        return (h + xx * mix_ref[p:p + 1]).astype(BF16)

    w0, a0, k_k, k_a = vec_ref[0:1], vec_ref[1:2], vec_ref[2:3], vec_ref[3:4]
    r = _dot(mixed(0), wrkv_ref[0])
    k = _dot(mixed(1), wrkv_ref[1])
    xv = mixed(2)
    v = _dot(xv, wrkv_ref[2])
    z = w0 + _dot(jnp.tanh(_dot(mixed(3), w1_ref[...])).astype(BF16), w2_ref[...])
    w = -(jnp.maximum(-z, 0.0) + jnp.log(1.0 + jnp.exp(-jnp.abs(z)))) - 0.5
    a = _sigmoid(a0 + _dot(_dot(mixed(4), a1_ref[...]).astype(BF16), a2_ref[...]))
    gate = _dot(_sigmoid(_dot(mixed(5), g1_ref[...])).astype(BF16), g2_ref[...])
    if has_vres:
        v0 = vec_ref[4:5]
        sv = _sigmoid(v0 + _dot(_dot(xv, v1_ref[...]).astype(BF16), v2_ref[...]))
        v = v + (vf_ref[...] - v) * sv
    r_ref[...] = r.astype(BF16)
    lw_ref[...] = -jnp.exp(w)
    k_ref[...] = (k * (1.0 + (a - 1.0) * k_a)).astype(BF16)
    v_ref[...] = v
    kk_ref[...] = (k * k_k).astype(BF16)
    a_ref[...] = a.astype(BF16)
    gate_ref[...] = gate.astype(BF16)


def _rwkv_pre(x2, gain, mix, wrkv, vecs, w1, w2, a1, a2, g1, g2, vres, *, tm, tiles_per_batch):
    n, d = x2.shape
    has_vres = vres is not None
    tok = pl.BlockSpec((tm, d), lambda i: (i, 0))
    prev = pl.BlockSpec((SUBLANES, d), lambda i: (jnp.maximum(i * (tm // SUBLANES) - 1, 0), 0))
    in_specs = [tok, prev, _const_spec(gain.shape), _const_spec(mix.shape), _const_spec(wrkv.shape),
                _const_spec(vecs.shape), _const_spec(w1.shape), _const_spec(w2.shape), _const_spec(a1.shape),
                _const_spec(a2.shape), _const_spec(g1.shape), _const_spec(g2.shape)]
    args = [x2, x2, gain, mix, wrkv, vecs, w1, w2, a1, a2, g1, g2]
    if has_vres:
        v_first, v1, v2 = vres
        in_specs += [tok, _const_spec(v1.shape), _const_spec(v2.shape)]
        args += [v_first, v1, v2]
    return pl.pallas_call(
        functools.partial(_rwkv_pre_kernel, tm=tm, tiles_per_batch=tiles_per_batch, has_vres=has_vres),
        grid=(n // tm,),
        in_specs=in_specs,
        out_specs=[tok] * 7,
        out_shape=[jax.ShapeDtypeStruct((n, d), dt) for dt in (BF16, F32, BF16, F32, BF16, BF16, BF16)],
        compiler_params=_params(),
        name="rwkv_pre",
    )(*args)


def _rwkv_scan_kernel(r_ref, lw_ref, k_ref, v_ref, kk_ref, a_ref, gate_ref, vec_ref, y_ref, s_ref, *, n_sub, bsz):
    L = RW_CHUNK

    @pl.when(pl.program_id(0) == 0)
    def _():
        s_ref[...] = jnp.zeros_like(s_ref)

    ti = lax.broadcasted_iota(jnp.int32, (L, LANES), 0)
    si = lax.broadcasted_iota(jnp.int32, (L, LANES), 1) % L
    mask_strict = ti > si
    mask_incl2 = jnp.concatenate([ti >= si, ti >= si], axis=1)
    eye = (ti == si).astype(F32)
    level_masks = []
    s = 1
    while s < L:
        level_masks.append(((ti // (2 * s)) == (si // (2 * s))) & ((ti // s) > (si // s)))
        s *= 2
    tri = (lax.broadcasted_iota(jnp.int32, (L, L), 0) >= lax.broadcasted_iota(jnp.int32, (L, L), 1)).astype(BF16)
    head0 = lax.broadcasted_iota(jnp.int32, (1, LANES), 1) < RW_HEAD_DIM
    same = ((lax.broadcasted_iota(jnp.int32, (LANES, LANES), 0) // RW_HEAD_DIM)
            == (lax.broadcasted_iota(jnp.int32, (LANES, LANES), 1) // RW_HEAD_DIM))

    def cumsum(z):
        hi = z.astype(BF16)
        r1 = z - hi.astype(F32)
        mid = r1.astype(BF16)
        lo = (r1 - mid.astype(F32)).astype(BF16)
        return _dot(tri, hi) + (_dot(tri, mid) + _dot(tri, lo))

    def stack(z):
        return jnp.concatenate([jnp.where(head0, z, 0.0), jnp.where(head0, 0.0, z)], axis=0).astype(BF16)

    def head_sum(z):
        s0 = jnp.sum(jnp.where(head0, z, 0.0), axis=-1, keepdims=True)
        s1 = jnp.sum(jnp.where(head0, 0.0, z), axis=-1, keepdims=True)
        return jnp.where(head0, s0, s1)

    items = [(b, slice(g * LANES, (g + 1) * LANES)) for b in range(bsz) for g in range(RW_GROUPS)]
    inv_n = 1.0 / RW_HEAD_DIM

    def phase_a(j):
        rows = slice(j * L, (j + 1) * L)
        lw_all = [lw_ref[b, rows, :] for b in range(bsz)]
        cum_all = [cumsum(z) for z in lw_all]
        cum = [cum_all[b][:, ln] for b, ln in items]
        w_in = [jnp.exp(c) for c in cum]
        w_ex = [jnp.exp(c - lw_all[b][:, ln]) for c, (b, ln) in zip(cum, items)]
        w_inv = [jnp.exp(-c) for c in cum]
        kk = [kk_ref[b, rows, ln].astype(F32) for b, ln in items]
        kk = [z / jnp.maximum(jnp.sqrt(head_sum(z * z)), NORM_EPS) for z in kk]
        r = [r_ref[b, rows, ln].astype(F32) for b, ln in items]
        k = [k_ref[b, rows, ln].astype(F32) for b, ln in items]
        x_ar = [jnp.concatenate([-z * we, rr * wi], axis=0).astype(BF16)
                for z, we, rr, wi in zip(kk, w_ex, r, w_in)]
        v = [v_ref[b, rows, ln] for b, ln in items]
        v_d = [stack(z) for z in v]
        bt = [z * a_ref[b, rows, ln].astype(F32) * w for z, w, (b, ln) in zip(kk, w_inv, items)]
        kt = [kx * w for kx, w in zip(k, w_inv)]
        aa = [_dot_nt(x, jnp.concatenate([stack(b_), stack(k_)], axis=0)) for x, b_, k_ in zip(x_ar, bt, kt)]
        a_ab = [jnp.where(mask_strict, z[:L, :LANES], 0.0) for z in aa]
        a_ak = [jnp.where(mask_strict, z[:L, LANES:], 0.0).astype(BF16) for z in aa]
        a_r = [jnp.where(mask_incl2, z[L:, :], 0.0).astype(BF16) for z in aa]
        tm = [eye + jnp.where(level_masks[0], a, 0.0) for a in a_ab]
        for lm in level_masks[1:]:
            ta = [_dot(t.astype(BF16), stack(jnp.where(lm, a, 0.0))).astype(BF16) for t, a in zip(tm, a_ab)]
            tm = [t + _dot(m, stack(t)) for t, m in zip(tm, ta)]
        bk = [jnp.concatenate([b_, k_], axis=0).astype(BF16) for b_, k_ in zip(bt, kt)]
        w_end = [w[L - 1:L, :] for w in w_in]
        rk = [rr * kx for rr, kx in zip(r, k)]
        return dict(x_ar=x_ar, v=v, v_d=v_d, a_ak=a_ak, a_r=a_r, tm=[t.astype(BF16) for t in tm], bk=bk,
                    w_end=w_end, rk=rk)

    def phase_b(pa, s_bd):
        xs = [_dot_nt(x, s.astype(BF16)) for x, s in zip(pa["x_ar"], s_bd)]
        z = [q[:L] + _dot(a, vd) for q, a, vd in zip(xs, pa["a_ak"], pa["v_d"])]
        u = [_dot(t, stack(zz)) for t, zz in zip(pa["tm"], z)]
        y = [q[L:] + _dot(ar, jnp.concatenate([stack(uu), vd], axis=0))
             for q, ar, uu, vd in zip(xs, pa["a_r"], u, pa["v_d"])]
        ds = [_dot_tn(jnp.concatenate([uu, vv], axis=0).astype(BF16), bk_) for uu, vv, bk_ in zip(u, pa["v"], pa["bk"])]
        s_new = [(s + jnp.where(same, d, 0.0)) * w for s, d, w in zip(s_bd, ds, pa["w_end"])]
        return y, s_new

    def phase_c(j, pa, y):
        rows = slice(j * L, (j + 1) * L)
        for i, (b, ln) in enumerate(items):
            yc = y[i] - head_sum(y[i]) * inv_n
            var = head_sum(yc * yc) * inv_n
            yn = yc * lax.rsqrt(var + RW_GN_EPS) * vec_ref[0:1, ln] + vec_ref[1:2, ln]
            bonus = head_sum(pa["rk"][i] * vec_ref[2:3, ln])
            out = (yn + bonus * pa["v"][i]) * gate_ref[b, rows, ln].astype(F32)
            y_ref[b, rows, ln] = out.astype(BF16)

    state = [s_ref[i] for i in range(len(items))]
    pa = phase_a(0)
    for j in range(n_sub):
        pa_next = phase_a(j + 1) if j + 1 < n_sub else None
        y, state = phase_b(pa, state)
        phase_c(j, pa, y)
        pa = pa_next
    for i in range(len(items)):
        s_ref[i] = state[i]


def _rwkv_scan(r, lw, k, v, kk, a, gate, vecs, *, n_sub):
    bsz, t, d = r.shape
    lb = n_sub * RW_CHUNK
    spec = pl.BlockSpec((bsz, lb, d), lambda j: (0, j, 0))
    return pl.pallas_call(
        functools.partial(_rwkv_scan_kernel, n_sub=n_sub, bsz=bsz),
        grid=(t // lb,),
        in_specs=[spec] * 7 + [_const_spec(vecs.shape)],
        out_specs=spec,
        out_shape=jax.ShapeDtypeStruct((bsz, t, d), BF16),
        scratch_shapes=[pltpu.VMEM((bsz * RW_GROUPS, LANES, LANES), F32)],
        compiler_params=_params(),
        name="rwkv_scan",
    )(r, lw, k, v, kk, a, gate, vecs)


def kernel(x, mem, norm_gains, mem_norm_gains, xa_wq, xa_wkv, xa_wo, ffn_w_in, ffn_w_out, gm_w_in, gm_b_in, gm_ln_g, gm_ln_b, gm_w_s, gm_b_s, gm_w_out, rw_mix, rw_w_rkv, rw_w0, rw_w1, rw_w2, rw_a0, rw_a1, rw_a2, rw_g1, rw_g2, rw_k_k, rw_k_a, rw_r_k, rw_ln_g, rw_ln_b, rw_w_o, rw_v0, rw_v1, rw_v2):
    bsz, seq, d = x.shape
    depth = norm_gains.shape[0]
    assert d == D_MODEL
    tm = min(512, seq)
    assert seq % tm == 0 and tm % GM_CHUNK == 0
    tiles_per_batch = seq // tm
    n_sub = min(4, seq // RW_CHUNK)
    assert seq % (n_sub * RW_CHUNK) == 0

    bf = lambda w: w.astype(BF16)
    kv_all = _memkv(mem, mem_norm_gains, bf(xa_wkv))

    x2 = x.reshape(bsz * seq, d)
    v_first = None
    for i in range(depth):
        g = norm_gains[i]
        j = i // 2
        mixer = None
        if i % 2 == 0:
            x2 = _gmlp(x2, g[0:2], bf(gm_w_in[j]), gm_b_in[j][None, :],
                       jnp.stack([gm_ln_g[j], gm_ln_b[j]]), bf(jnp.tril(gm_w_s[j])),
                       jnp.repeat(gm_b_s[j].T, GM_GROUP_DIM, axis=1), bf(gm_w_out[j]), tm=tm)
        else:
            has_vres = j > 0
            vec_rows = [rw_w0[j], rw_a0[j], rw_k_k[j], rw_k_a[j]]
            vres = None
            if has_vres:
                vec_rows.append(rw_v0[j - 1])
                vres = (v_first, bf(rw_v1[j - 1]), bf(rw_v2[j - 1]))
            pre = _rwkv_pre(
                x2, g[0:1], rw_mix[j], bf(rw_w_rkv[j]), jnp.stack(vec_rows), bf(rw_w1[j]), bf(rw_w2[j]),
                bf(rw_a1[j]), bf(rw_a2[j]), bf(rw_g1[j]), bf(rw_g2[j]), vres,
                tm=tm, tiles_per_batch=tiles_per_batch)
            if not has_vres:
                v_first = pre[3]
            y = _rwkv_scan(*[z.reshape(bsz, seq, d) for z in pre],
                           jnp.stack([rw_ln_g[j], rw_ln_b[j], rw_r_k[j].reshape(d)]), n_sub=n_sub)
            mixer = (y.reshape(bsz * seq, d), g[1:2], bf(rw_w_o[j]))
        x2 = _xattn(x2, g[2:4], bf(xa_wq[i]), kv_all[i], bf(xa_wo[i]), tm=tm, tiles_per_batch=tiles_per_batch,
                    mixer=mixer)
        x2 = _ffn(x2, g[4:6], bf(ffn_w_in[i]), bf(ffn_w_out[i]), tm=tm)
    return x2.reshape(bsz, seq, d)
```
